```python
import jax, jax.numpy as jnp
from jax import lax
import numpy as np

D_MODEL = 2048
BATCH = 2
SEQ = 4096
DEPTH = 1
DEC_BATCH = 128
DEC_SEQ = 1
PAST_LEN = 16384
PAGE_SIZE = 128

N_META = 16
MIX_WIDTH = D_MODEL
GLA_HEADS = 4
GLA_WIDTH = MIX_WIDTH // 2
GLA_DV = GLA_WIDTH // GLA_HEADS
GLA_DK = GLA_DV // 2
GLA_KW = GLA_HEADS * GLA_DK
GLA_GATE_RANK = 16
GLA_GATE_TAU = 16.0
GLA_CHUNK = 64
MLA_VDIM = 128
MLA_WIDTH = MIX_WIDTH - GLA_WIDTH
MLA_HEADS = MLA_WIDTH // MLA_VDIM
MLA_NOPE = 128
MLA_ROPE = 64
MLA_Q_RANK = 512
MLA_KV_RANK = 512
MLA_SCALE = (MLA_NOPE + MLA_ROPE) ** -0.5
ROPE_BASE = 10000.0
Q_BLOCK = 128
D_FF = 4 * D_MODEL
LN_EPS = 1e-5
RMS_EPS = 1e-6
DEEPNORM_ALPHA = (2 * DEPTH) ** 0.25
DEEPNORM_BETA = (8 * DEPTH) ** -0.25
IN_SPLITS = (GLA_KW, GLA_KW, GLA_WIDTH, GLA_WIDTH, GLA_GATE_RANK, MLA_Q_RANK, MLA_KV_RANK, MLA_ROPE)
D_IN = sum(IN_SPLITS)

kernel_name = 'hymba_gla_mla_deepnorm_step'


def layer_norm(x, g, b):
    xf = x.astype(jnp.float32)
    mu = jnp.mean(xf, axis=-1, keepdims=True)
    var = jnp.mean(jnp.square(xf - mu), axis=-1, keepdims=True)
    return ((xf - mu) * lax.rsqrt(var + LN_EPS) * g + b).astype(x.dtype)


def rms_norm(x, g):
    xf = x.astype(jnp.float32)
    return (xf * lax.rsqrt(jnp.mean(xf * xf, axis=-1, keepdims=True) + RMS_EPS) * g).astype(x.dtype)


def rope(x, pos):
    half = MLA_ROPE // 2
    inv_freq = ROPE_BASE ** (-jnp.arange(half, dtype=jnp.float32) / half)
    ang = pos.astype(jnp.float32)[:, None] * inv_freq[None, :]
    cos, sin = jnp.cos(ang), jnp.sin(ang)
    x1 = x[..., :half].astype(jnp.float32)
    x2 = x[..., half:].astype(jnp.float32)
    return jnp.concatenate([x1 * cos - x2 * sin, x1 * sin + x2 * cos], axis=-1).astype(x.dtype)


def split_in(z):
    parts = []
    start = 0
    for width in IN_SPLITS:
        parts.append(z[..., start:start + width])
        start += width
    return parts


def project(h, p, pos):
    bsz, t = h.shape[0], h.shape[1]
    z = h @ p['w_in']
    gq, gk, gv, gg, ga, cq, ckv, kpe = split_in(z)

    def heads(a, n):
        return a.reshape(bsz, t, n, -1).transpose(0, 2, 1, 3)

    q_gla = heads(gq, GLA_HEADS) * GLA_DK ** -0.5
    k_gla = heads(gk, GLA_HEADS)
    v_gla = heads(gv, GLA_HEADS)
    log_a = jax.nn.log_sigmoid((ga @ p['w_gla_a2'] + p['b_gla_a']).astype(jnp.float32)) / GLA_GATE_TAU
    log_a = heads(log_a, GLA_HEADS)
    q = (rms_norm(cq, p['mla_q_norm_g']) @ p['w_mla_uq']).reshape(bsz, t, MLA_HEADS, MLA_NOPE + MLA_ROPE)
    q_lat = jnp.einsum('bthn,chn->bhtc', q[..., :MLA_NOPE], p['w_mla_ukv'][..., :MLA_NOPE])
    q_pe = rope(q[..., MLA_NOPE:].transpose(0, 2, 1, 3), pos)
    ckv_n = rms_norm(ckv, p['mla_kv_norm_g'])
    kpe_r = rope(kpe, pos)
    return q_gla, k_gla, v_gla, log_a, gg, q_lat, q_pe, ckv_n, kpe_r


def gla_update(s, k, v, b):
    b_last = b[:, :, -1:, :]
    k_dec = k * jnp.exp(b_last - b)
    return jnp.exp(b_last[:, :, 0, :, None]) * s + jnp.einsum('bhck,bhcv->bhkv', k_dec, v)


def gla_output(s, q, k, v, b):
    c = q.shape[2]
    causal = jnp.tril(jnp.ones((c, c), dtype=bool))
    diff = b[:, :, :, None, :] - b[:, :, None, :, :]
    decay = jnp.exp(jnp.where(causal[:, :, None], diff, -jnp.inf))
    a_intra = jnp.einsum('bhtk,bhsk,bhtsk->bhts', q, k, decay)
    return jnp.einsum('bhtk,bhkv->bhtv', q * jnp.exp(b), s) + jnp.einsum('bhts,bhsv->bhtv', a_intra, v)


def gla_prompt(q, k, v, log_a, keep_meta):
    bsz, nh, length = q.shape[0], q.shape[1], q.shape[2]
    s0 = jnp.zeros((bsz, nh, GLA_DK, GLA_DV), jnp.float32)
    qm, km, vm, lam = q[:, :, :N_META], k[:, :, :N_META], v[:, :, :N_META], log_a[:, :, :N_META]
    bm = jnp.cumsum(lam, axis=2)
    s_meta = gla_update(s0, km, vm, bm)

    def chunks(a):
        return a[:, :, N_META:].reshape(bsz, nh, -1, GLA_CHUNK, a.shape[-1]).transpose(2, 0, 1, 3, 4)

    def step(s, blk):
        qc, kc, vc, lac = blk
        bc = jnp.cumsum(lac, axis=2)
        return gla_update(s, kc, vc, bc), gla_output(s, qc, kc, vc, bc)

    s_fin, o = lax.scan(step, s_meta, (chunks(q), chunks(k), chunks(v), chunks(log_a)))
    o = o.transpose(1, 2, 0, 3, 4).reshape(bsz, nh, length - N_META, GLA_DV)
    if keep_meta:
        o = jnp.concatenate([gla_output(s0, qm, km, vm, bm), o], axis=2)
    return o, s_fin


def mla_scores(q_lat, q_pe, ckv, kpe):
    s = jnp.einsum('bhtc,bsc->bhts', q_lat, ckv) + jnp.einsum('bhtr,bsr->bhts', q_pe, kpe)
    return s.astype(jnp.float32) * MLA_SCALE


def mla_prompt(q_lat, q_pe, ckv, kpe, pos, keep_meta):
    bsz, nh, length = q_lat.shape[0], q_lat.shape[1], q_lat.shape[2]
    nb = (length - N_META) // Q_BLOCK

    def blocks(a):
        return a[:, :, N_META:].reshape(bsz, nh, nb, Q_BLOCK, a.shape[-1]).transpose(2, 0, 1, 3, 4)

    def attend(args):
        ql, qp, qpos = args
        s = mla_scores(ql, qp, ckv, kpe)
        s = jnp.where(pos[None, :] <= qpos[:, None], s, -jnp.inf)
        prob = jax.nn.softmax(s, axis=-1)
        return jnp.einsum('bhts,bsc->bhtc', prob, ckv)

    o = lax.map(attend, (blocks(q_lat), blocks(q_pe), pos[N_META:].reshape(nb, Q_BLOCK)))
    o = o.transpose(1, 2, 0, 3, 4).reshape(bsz, nh, length - N_META, MLA_KV_RANK)
    if keep_meta:
        o_meta = attend((q_lat[:, :, :N_META], q_pe[:, :, :N_META], pos[:N_META]))
        o = jnp.concatenate([o_meta, o], axis=2)
    return o


def mla_sample(q_lat, q_pe, ckv_new, kpe_new, cache_ckv, cache_kpe, page_table, layer):
    t = q_lat.shape[2]
    s = mla_scores(q_lat, q_pe, ckv_new, kpe_new)
    s = jnp.where(jnp.tril(jnp.ones((t, t), dtype=bool)), s, -jnp.inf)
    m = jnp.max(s, axis=-1)
    prob = jnp.exp(s - m[..., None])
    carry = (m, jnp.sum(prob, axis=-1), jnp.einsum('bhts,bsc->bhtc', prob, ckv_new))

    def step(carry, pages):
        m, l, acc = carry
        ck = cache_ckv[layer, pages]
        kp = cache_kpe[layer, pages]
        s = mla_scores(q_lat, q_pe, ck, kp)
        m_new = jnp.maximum(m, jnp.max(s, axis=-1))
        corr = jnp.exp(m - m_new)
        prob = jnp.exp(s - m_new[..., None])
        acc = acc * corr[..., None] + jnp.einsum('bhts,bsc->bhtc', prob, ck)
        return (m_new, l * corr + jnp.sum(prob, axis=-1), acc), None

    (m, l, acc), _ = lax.scan(step, carry, page_table.T)
    return acc / l[..., None]


def finish(h, o_gla, g, o_lat, p):
    bsz, t = h.shape[0], h.shape[1]
    o_gla = rms_norm(o_gla, p['gla_norm_g'])
    o_gla = o_gla.transpose(0, 2, 1, 3).reshape(bsz, t, GLA_WIDTH) * jax.nn.silu(g)
    o_mla = jnp.einsum('bhtc,chv->bthv', o_lat, p['w_mla_ukv'][..., MLA_NOPE:]).reshape(bsz, t, MLA_WIDTH)
    mix = jnp.concatenate([o_gla, o_mla], axis=-1).astype(h.dtype) @ p['w_out']
    h = layer_norm(DEEPNORM_ALPHA * h + mix, p['ln1_g'], p['ln1_b'])
    ff = jnp.square(jax.nn.relu(h @ p['w_up'])) @ p['w_down']
    return layer_norm(DEEPNORM_ALPHA * h + ff, p['ln2_g'], p['ln2_b'])


def setup_inputs(seed: int = 0) -> dict:
    key = jax.random.key(seed)
    ks = jax.random.split(key, 24)
    f32 = jnp.float32
    n_pages = PAST_LEN // PAGE_SIZE
    n_used = DEC_BATCH * n_pages
    n_phys = n_used + max(1, n_used // 4)

    def nrm(k, shape, scale=1.0):
        return jax.random.normal(k, shape, f32) * scale

    def gain(k, shape):
        return 1.0 + 0.02 * jax.random.normal(k, shape, f32)

    page_table = jax.random.permutation(ks[5], n_phys)[:n_used].reshape(DEC_BATCH, n_pages).astype(jnp.int32)
    return {
        'x_prompt': nrm(ks[0], (BATCH, SEQ, D_MODEL)),
        'x_sample': nrm(ks[1], (DEC_BATCH, DEC_SEQ, D_MODEL)),
        'cache_ckv': nrm(ks[2], (DEPTH, n_phys, PAGE_SIZE, MLA_KV_RANK)),
        'cache_kpe': nrm(ks[3], (DEPTH, n_phys, PAGE_SIZE, MLA_ROPE)),
        'state_gla': nrm(ks[4], (DEPTH, DEC_BATCH, GLA_HEADS, GLA_DK, GLA_DV)),
        'page_table': page_table,
        'meta_tokens': nrm(ks[6], (N_META, D_MODEL)),
        'ln_emb_g': gain(ks[7], (D_MODEL,)),
        'ln_emb_b': nrm(ks[8], (D_MODEL,), 0.02),
        'w_in': nrm(ks[9], (DEPTH, D_MODEL, D_IN), D_MODEL ** -0.5),
        'w_gla_a2': nrm(ks[10], (DEPTH, GLA_GATE_RANK, GLA_KW), GLA_GATE_RANK ** -0.5),
        'b_gla_a': nrm(ks[11], (DEPTH, GLA_KW), 0.1),
        'gla_norm_g': gain(ks[12], (DEPTH, GLA_DV)),
        'mla_q_norm_g': gain(ks[13], (DEPTH, MLA_Q_RANK)),
        'w_mla_uq': nrm(ks[14], (DEPTH, MLA_Q_RANK, MLA_HEADS * (MLA_NOPE + MLA_ROPE)), MLA_Q_RANK ** -0.5),
        'mla_kv_norm_g': gain(ks[15], (DEPTH, MLA_KV_RANK)),
        'w_mla_ukv': nrm(ks[16], (DEPTH, MLA_KV_RANK, MLA_HEADS, MLA_NOPE + MLA_VDIM), MLA_KV_RANK ** -0.5),
        'w_out': nrm(ks[17], (DEPTH, MIX_WIDTH, D_MODEL), MIX_WIDTH ** -0.5 * DEEPNORM_BETA),
        'ln1_g': gain(ks[18], (DEPTH, D_MODEL)),
        'ln1_b': nrm(ks[19], (DEPTH, D_MODEL), 0.02),
        'w_up': nrm(ks[20], (DEPTH, D_MODEL, D_FF), D_MODEL ** -0.5),
        'w_down': nrm(ks[21], (DEPTH, D_FF, D_MODEL), D_FF ** -0.5 * DEEPNORM_BETA),
        'ln2_g': gain(ks[22], (DEPTH, D_MODEL)),
        'ln2_b': nrm(ks[23], (DEPTH, D_MODEL), 0.02),
    }


def reference(x_prompt, x_sample, cache_ckv, cache_kpe, state_gla, page_table,
              meta_tokens, ln_emb_g, ln_emb_b, w_in, w_gla_a2, b_gla_a, gla_norm_g,
              mla_q_norm_g, w_mla_uq, mla_kv_norm_g, w_mla_ukv, w_out,
              ln1_g, ln1_b, w_up, w_down, ln2_g, ln2_b):
    bsz = x_prompt.shape[0]
    meta = jnp.broadcast_to(meta_tokens[None].astype(x_prompt.dtype), (bsz, N_META, D_MODEL))
    hp = layer_norm(jnp.concatenate([meta, x_prompt], axis=1), ln_emb_g, ln_emb_b)
    hs = layer_norm(x_sample, ln_emb_g, ln_emb_b)
    pos_p = jnp.arange(hp.shape[1], dtype=jnp.int32)
    pos_s = PAST_LEN + jnp.arange(x_sample.shape[1], dtype=jnp.int32)

    ckv_p_l, kpe_p_l, gla_p_l, ckv_s_l, kpe_s_l, gla_s_l = [], [], [], [], [], []
    for layer in range(DEPTH):
        keep_meta = layer < DEPTH - 1
        p = dict(w_in=w_in[layer], w_gla_a2=w_gla_a2[layer], b_gla_a=b_gla_a[layer],
                 gla_norm_g=gla_norm_g[layer], mla_q_norm_g=mla_q_norm_g[layer],
                 w_mla_uq=w_mla_uq[layer], mla_kv_norm_g=mla_kv_norm_g[layer],
                 w_mla_ukv=w_mla_ukv[layer], w_out=w_out[layer], ln1_g=ln1_g[layer],
                 ln1_b=ln1_b[layer], w_up=w_up[layer], w_down=w_down[layer],
                 ln2_g=ln2_g[layer], ln2_b=ln2_b[layer])

        qg, kg, vg, la, g, q_lat, q_pe, ckv_p, kpe_p = project(hp, p, pos_p)
        o_gla, s_p = gla_prompt(qg, kg, vg, la, keep_meta)
        o_lat = mla_prompt(q_lat, q_pe, ckv_p, kpe_p, pos_p, keep_meta)
        start = 0 if keep_meta else N_META
        hp = finish(hp[:, start:], o_gla, g[:, start:], o_lat, p)

        qg, kg, vg, la, g, q_lat, q_pe, ckv_s, kpe_s = project(hs, p, pos_s)
        b = jnp.cumsum(la, axis=2)
        s_old = state_gla[layer]
        o_gla = gla_output(s_old, qg, kg, vg, b)
        s_s = gla_update(s_old, kg, vg, b)
        o_lat = mla_sample(q_lat, q_pe, ckv_s, kpe_s, cache_ckv, cache_kpe, page_table, layer)
        hs = finish(hs, o_gla, g, o_lat, p)

        ckv_p_l.append(ckv_p)
        kpe_p_l.append(kpe_p)
        gla_p_l.append(s_p)
        ckv_s_l.append(ckv_s)
        kpe_s_l.append(kpe_s)
        gla_s_l.append(s_s)

    new_ckv_prompt = jnp.stack(ckv_p_l)
    new_kpe_prompt = jnp.stack(kpe_p_l)
    new_gla_prompt = jnp.stack(gla_p_l)
    new_ckv_sample = jnp.stack(ckv_s_l)
    new_kpe_sample = jnp.stack(kpe_s_l)
    new_gla_sample = jnp.stack(gla_s_l)
    return (hp, hs, new_ckv_prompt, new_kpe_prompt, new_gla_prompt, new_ckv_sample, new_kpe_sample, new_gla_sample)
```

```python
import functools

import jax
import jax.numpy as jnp
from jax import lax
from jax.experimental import pallas as pl
from jax.experimental.pallas import tpu as pltpu

F32 = jnp.float32
BF16 = jnp.bfloat16

LN_EPS = 1e-5
RMS_EPS = 1e-6
ROPE_BASE = 10000.0
GLA_GATE_TAU = 16.0
GLA_CHUNK = 64
GLA_SAFE_RANGE = 60.0
LANES = 128
VMEM_LIMIT_BYTES = 56 * 1024 * 1024
DECODE_PAGES_PER_STEP = 8


def _params(*sem):
    return pltpu.CompilerParams(dimension_semantics=sem, vmem_limit_bytes=VMEM_LIMIT_BYTES)


def _layer_norm(x, g, b):
    mu = jnp.mean(x, axis=-1, keepdims=True)
    xc = x - mu
    var = jnp.mean(xc * xc, axis=-1, keepdims=True)
    return xc * lax.rsqrt(var + LN_EPS) * g + b


def _rms_norm(x, g):
    return x * lax.rsqrt(jnp.mean(x * x, axis=-1, keepdims=True) + RMS_EPS) * g


def _full(shape):
    return pl.BlockSpec(shape, lambda *_: (0,) * len(shape))


def _proj_gla_kernel(x_ref, lng_ref, lnb_ref, w_ref, wa2_ref, ba_ref,
                     q_ref, k_ref, v_ref, g_ref, la_ref, *, kw, gw, q_scale):
    h = _layer_norm(x_ref[...], lng_ref[...], lnb_ref[...]).astype(BF16)
    z = jnp.dot(h, w_ref[...], preferred_element_type=F32)
    q_ref[...] = z[:, :kw] * q_scale
    k_ref[...] = z[:, kw:2 * kw]
    v_ref[...] = z[:, 2 * kw:2 * kw + gw].astype(BF16)
    g_ref[...] = z[:, 2 * kw + gw:2 * kw + 2 * gw]
    ga = z[:, 2 * kw + 2 * gw:].astype(BF16)
    a = jnp.dot(ga, wa2_ref[...], preferred_element_type=F32) + ba_ref[...]
    la_ref[...] = jax.nn.log_sigmoid(a) * (1.0 / GLA_GATE_TAU)


def _proj_gla(x, lng, lnb, w, wa2, ba, *, tm, kw, gw, q_scale):
    m, d = x.shape
    row = lambda n: pl.BlockSpec((tm, n), lambda i: (i, 0))
    return pl.pallas_call(
        functools.partial(_proj_gla_kernel, kw=kw, gw=gw, q_scale=q_scale),
        grid=(m // tm,),
        in_specs=[row(d), _full(lng.shape), _full(lnb.shape), _full(w.shape), _full(wa2.shape), _full(ba.shape)],
        out_specs=[row(kw), row(kw), row(gw), row(gw), row(kw)],
        out_shape=[jax.ShapeDtypeStruct((m, kw), F32), jax.ShapeDtypeStruct((m, kw), F32),
                   jax.ShapeDtypeStruct((m, gw), BF16), jax.ShapeDtypeStruct((m, gw), F32),
                   jax.ShapeDtypeStruct((m, kw), F32)],
        compiler_params=_params("parallel"),
        name="proj_gla",
    )(x, lng, lnb, w, wa2, ba)


def _proj_mla_kernel(x_ref, lng_ref, lnb_ref, w_ref, qg_ref, kvg_ref, wuq_ref, wn_ref, cos_ref, sin_ref,
                     q_ref, ckv_ref, kpe_ref, kcat_ref, *, qr, kvr, rope, heads, nope, scale):
    h = _layer_norm(x_ref[...], lng_ref[...], lnb_ref[...]).astype(BF16)
    z = jnp.dot(h, w_ref[...], preferred_element_type=F32)
    cqn = _rms_norm(z[:, :qr], qg_ref[...]).astype(BF16)
    ckvn = _rms_norm(z[:, qr:qr + kvr], kvg_ref[...])
    cos = cos_ref[...]
    sin = sin_ref[...]
    kpe = z[:, qr + kvr:qr + kvr + rope]
    kpe_rot = z[:, qr + kvr + rope:]
    kpe_r = kpe * cos[:, :rope] + kpe_rot * sin[:, :rope]
    ckv_ref[...] = ckvn
    kpe_ref[...] = kpe_r
    kcat_ref[:, :kvr] = ckvn.astype(BF16)
    kcat_ref[:, kvr:] = kpe_r.astype(BF16)
    qf = jnp.dot(cqn, wuq_ref[...], preferred_element_type=F32)
    hn = heads * nope
    hr = heads * rope
    q_pe = (qf[:, hn:hn + hr] * cos + qf[:, hn + hr:] * sin) * scale
    for hd in range(heads):
        q_lat = jnp.dot(qf[:, hd * nope:(hd + 1) * nope].astype(BF16), wn_ref[hd],
                        preferred_element_type=F32) * scale
        q_ref[hd, :, :kvr] = q_lat.astype(BF16)
        q_ref[hd, :, kvr:] = q_pe[:, hd * rope:(hd + 1) * rope].astype(BF16)


def _proj_mla(x, lng, lnb, w, qg, kvg, wuq, wn, cos8, sin8, *, tm, scale):
    m, d = x.shape
    heads, nope, kvr = wn.shape
    qr = qg.shape[-1]
    rope = cos8.shape[-1] // heads
    n_pos_blocks = cos8.shape[0] // tm
    row = lambda n: pl.BlockSpec((tm, n), lambda i: (i, 0))
    tab = pl.BlockSpec((tm, heads * rope), lambda i: (i % n_pos_blocks, 0))
    return pl.pallas_call(
        functools.partial(_proj_mla_kernel, qr=qr, kvr=kvr, rope=rope, heads=heads, nope=nope, scale=scale),
        grid=(m // tm,),
        in_specs=[row(d), _full(lng.shape), _full(lnb.shape), _full(w.shape), _full(qg.shape), _full(kvg.shape),
                  _full(wuq.shape), _full(wn.shape), tab, tab],
        out_specs=[pl.BlockSpec((heads, tm, kvr + rope), lambda i: (0, i, 0)), row(kvr), row(rope), row(kvr + rope)],
        out_shape=[jax.ShapeDtypeStruct((heads, m, kvr + rope), BF16), jax.ShapeDtypeStruct((m, kvr), F32),
                   jax.ShapeDtypeStruct((m, rope), F32), jax.ShapeDtypeStruct((m, kvr + rope), BF16)],
        compiler_params=_params("parallel"),
        name="proj_mla",
    )(x, lng, lnb, w, qg, kvg, wuq, wn, cos8, sin8)


def _mla_prompt_kernel(q_ref, k_ref, kmeta_ref, o_ref, m_sc, l_sc, acc_sc, *, tq, kvr):
    i = pl.program_id(1)
    heads = q_ref.shape[0]
    rows = heads * tq
    q = q_ref[...].reshape(rows, q_ref.shape[2])
    nt = (((1,), (1,)), ((), ()))

    km = kmeta_ref[...]
    s = lax.dot_general(q, km, nt, preferred_element_type=F32)
    m0 = jnp.max(s, axis=-1, keepdims=True)
    p = jnp.exp(s - m0)
    m_sc[...] = m0
    l_sc[...] = jnp.sum(p, axis=-1, keepdims=True)
    acc_sc[...] = jnp.dot(p.astype(BF16), km[:, :kvr], preferred_element_type=F32)

    def update(s, kblk):
        m_prev = m_sc[...]
        m_new = jnp.maximum(m_prev, jnp.max(s, axis=-1, keepdims=True))
        corr = jnp.exp(m_prev - m_new)
        p = jnp.exp(s - m_new)
        l_sc[...] = l_sc[...] * corr + jnp.sum(p, axis=-1, keepdims=True)
        acc_sc[...] = acc_sc[...] * corr + jnp.dot(p.astype(BF16), kblk[:, :kvr], preferred_element_type=F32)
        m_sc[...] = m_new

    def body(j, carry):
        kblk = k_ref[pl.ds(pl.multiple_of(j * tq, tq), tq), :]
        update(lax.dot_general(q, kblk, nt, preferred_element_type=F32), kblk)
        return carry

    lax.fori_loop(0, i, body, 0)

    kblk = k_ref[pl.ds(pl.multiple_of(i * tq, tq), tq), :]
    s = lax.dot_general(q, kblk, nt, preferred_element_type=F32)
    tok = lax.broadcasted_iota(jnp.int32, (heads, tq, tq), 1).reshape(rows, tq)
    key = lax.broadcasted_iota(jnp.int32, (rows, tq), 1)
    update(jnp.where(key <= tok, s, -jnp.inf), kblk)

    o = acc_sc[...] / l_sc[...]
    o_ref[...] = o.reshape(heads, tq, kvr).astype(o_ref.dtype)


def _mla_prompt(q, kcat, kmeta, *, bsz, seq, tq, kvr):
    heads, m, dq = q.shape
    nq = seq // tq
    rows = heads * tq
    return pl.pallas_call(
        functools.partial(_mla_prompt_kernel, tq=tq, kvr=kvr),
        grid=(bsz, nq),
        in_specs=[pl.BlockSpec((heads, tq, dq), lambda b, i: (0, b * nq + i, 0)),
                  pl.BlockSpec((seq, dq), lambda b, i: (b, 0)),
                  _full(kmeta.shape)],
        out_specs=pl.BlockSpec((heads, tq, kvr), lambda b, i: (0, b * nq + i, 0)),
        out_shape=jax.ShapeDtypeStruct((heads, m, kvr), BF16),
        scratch_shapes=[pltpu.VMEM((rows, 1), F32), pltpu.VMEM((rows, 1), F32), pltpu.VMEM((rows, kvr), F32)],
        compiler_params=_params("parallel", "arbitrary"),
        name="mla_prompt",
    )(q, kcat, kmeta)


def _split3(x):
    hi = x.astype(BF16)
    r1 = x - hi.astype(F32)
    mid = r1.astype(BF16)
    lo = (r1 - mid.astype(F32)).astype(BF16)
    return hi, mid, lo


def _cumsum_rows(x):
    c = x.shape[0]
    tri = (lax.broadcasted_iota(jnp.int32, (c, c), 1) <= lax.broadcasted_iota(jnp.int32, (c, c), 0)).astype(BF16)
    hi, mid, lo = _split3(x)
    dot = lambda t: jnp.dot(tri, t, preferred_element_type=F32)
    return dot(hi) + dot(mid) + dot(lo)


def _gla_state_update(s_ref, kd, v, decay_last):
    c, dk = kd.shape
    aug = jnp.concatenate([kd, jnp.broadcast_to(decay_last, (8, dk)), jnp.zeros((dk - c - 8, dk), F32)], axis=0)
    aug_t = jnp.transpose(aug)
    kd_t = aug_t[:, :c].astype(BF16)
    decay_col = aug_t[:, c:c + 1]
    s_ref[...] = s_ref[...] * decay_col + jnp.dot(kd_t, v, preferred_element_type=F32)


def _gla_prompt_kernel(q_ref, k_ref, v_ref, la_ref, km_ref, vm_ref, lam_ref, o_ref, sfin_ref, s_sc, *, chunk):
    seq = q_ref.shape[0]
    n_chunks = seq // chunk

    bm = _cumsum_rows(lam_ref[...])
    bm_last = bm[-1:, :]
    s_sc[...] = jnp.zeros_like(s_sc)
    _gla_state_update(s_sc, km_ref[...] * jnp.exp(bm_last - bm), vm_ref[...], jnp.exp(bm_last))

    la_all = la_ref[...]
    chunk_decay = -jnp.sum(la_all.reshape(n_chunks, chunk, la_all.shape[1]), axis=1)
    safe = jnp.max(chunk_decay) <= GLA_SAFE_RANGE

    def load(c):
        r0 = pl.multiple_of(c * chunk, chunk)
        rows = pl.ds(r0, chunk)
        return rows, q_ref[rows, :], k_ref[rows, :], v_ref[rows, :], la_ref[rows, :]

    causal = lax.broadcasted_iota(jnp.int32, (chunk, chunk), 1) <= lax.broadcasted_iota(jnp.int32, (chunk, chunk), 0)

    def fast_chunk(c, carry):
        rows, q, k, v, la = load(c)
        b = _cumsum_rows(la)
        b_last = b[-1:, :]
        o_inter = jnp.dot((q * jnp.exp(b)).astype(BF16), s_sc[...].astype(BF16), preferred_element_type=F32)
        kd = k * jnp.exp(b_last - b)
        qd = q * jnp.exp(b - b_last)
        a = lax.dot_general(qd.astype(BF16), kd.astype(BF16), (((1,), (1,)), ((), ())), preferred_element_type=F32)
        a = jnp.where(causal, a, 0.0)
        o_ref[rows, :] = o_inter + jnp.dot(a.astype(BF16), v, preferred_element_type=F32)
        _gla_state_update(s_sc, kd, v, jnp.exp(b_last))
        return carry

    def exact_chunk(c, carry):
        rows, q, k, v, la = load(c)
        r0 = pl.multiple_of(c * chunk, chunk)
        b = _cumsum_rows(la)
        b_last = b[-1:, :]
        o_ref[rows, :] = jnp.dot((q * jnp.exp(b)).astype(BF16), s_sc[...].astype(BF16), preferred_element_type=F32)
        vf = v.astype(F32)
        row_id = lax.broadcasted_iota(jnp.int32, (chunk, 1), 0)

        def token(t, carry2):
            sel = row_id == t
            bt = jnp.sum(jnp.where(sel, b, 0.0), axis=0, keepdims=True)
            qt = jnp.sum(jnp.where(sel, q, 0.0), axis=0, keepdims=True)
            w = jnp.exp(jnp.where(row_id <= t, bt - b, -jnp.inf))
            a_col = jnp.sum(w * qt * k, axis=-1, keepdims=True)
            o_t = jnp.sum(a_col * vf, axis=0, keepdims=True)
            o_ref[pl.ds(r0 + t, 1), :] = o_ref[pl.ds(r0 + t, 1), :] + o_t
            return carry2

        lax.fori_loop(0, chunk, token, 0)
        _gla_state_update(s_sc, k * jnp.exp(b_last - b), v, jnp.exp(b_last))
        return carry

    @pl.when(safe)
    def _():
        lax.fori_loop(0, n_chunks, fast_chunk, 0)

    @pl.when(jnp.logical_not(safe))
    def _():
        lax.fori_loop(0, n_chunks, exact_chunk, 0)

    sfin_ref[...] = s_sc[...]


def _gla_prompt(q, k, v, la, km, vm, lam, *, bsz, seq, heads, dk, dv):
    m = q.shape[0]
    nm = km.shape[0]
    col = lambda n: pl.BlockSpec((seq, n), lambda b, h: (b, h))
    mcol = lambda n: pl.BlockSpec((nm, n), lambda b, h: (0, h))
    return pl.pallas_call(
        functools.partial(_gla_prompt_kernel, chunk=GLA_CHUNK),
        grid=(bsz, heads),
        in_specs=[col(dk), col(dk), col(dv), col(dk), mcol(dk), mcol(dv), mcol(dk)],
        out_specs=[pl.BlockSpec((None, seq, dv), lambda b, h: (h, b, 0)),
                   pl.BlockSpec((None, None, dk, dv), lambda b, h: (b, h, 0, 0))],
        out_shape=[jax.ShapeDtypeStruct((heads, m, dv), F32), jax.ShapeDtypeStruct((bsz, heads, dk, dv), F32)],
        scratch_shapes=[pltpu.VMEM((dk, dv), F32)],
        compiler_params=_params("parallel", "parallel"),
        name="gla_prompt",
    )(q, k, v, la, km, vm, lam)


def _mla_decode_kernel(pt_ref, q_ref, ckvs_ref, kpes_ref, *refs, pages, kvr):
    del pt_ref
    ckv_refs = refs[:pages]
    kpe_refs = refs[pages:2 * pages]
    o_ref, m_sc, l_sc, acc_sc = refs[2 * pages:]
    j = pl.program_id(1)
    q = q_ref[0]
    heads = q.shape[0]

    @pl.when(j == 0)
    def _():
        ckv_new = ckvs_ref[0]
        kpe_new = kpes_ref[0]
        s0 = (jnp.sum(q[:, :kvr] * ckv_new, axis=-1, keepdims=True)
              + jnp.sum(q[:, kvr:] * kpe_new, axis=-1, keepdims=True))
        m_sc[...] = s0
        l_sc[...] = jnp.ones_like(l_sc)
        acc_sc[...] = jnp.broadcast_to(ckv_new, (heads, kvr))

    qb = q.astype(BF16)
    ck = jnp.concatenate([r[...].astype(BF16) for r in ckv_refs], axis=0)
    kp = jnp.concatenate([r[...].astype(BF16) for r in kpe_refs], axis=0)
    nt = (((1,), (1,)), ((), ()))
    s = (lax.dot_general(qb[:, :kvr], ck, nt, preferred_element_type=F32)
         + lax.dot_general(qb[:, kvr:], kp, nt, preferred_element_type=F32))
    m_prev = m_sc[...]
    m_new = jnp.maximum(m_prev, jnp.max(s, axis=-1, keepdims=True))
    corr = jnp.exp(m_prev - m_new)
    p = jnp.exp(s - m_new)
    l_sc[...] = l_sc[...] * corr + jnp.sum(p, axis=-1, keepdims=True)
    acc_sc[...] = acc_sc[...] * corr + jnp.dot(p.astype(BF16), ck, preferred_element_type=F32)
    m_sc[...] = m_new

    @pl.when(j == pl.num_programs(1) - 1)
    def _():
        o_ref[0] = acc_sc[...] / l_sc[...]


def _mla_decode(page_table, q, ckv_new, kpe_new, cache_ckv, cache_kpe, *, pages):
    nb, heads, dq = q.shape
    n_pages = page_table.shape[1]
    page, kvr = cache_ckv.shape[2:]
    rope = cache_kpe.shape[3]
    steps = n_pages // pages

    def page_spec(width, p):
        return pl.BlockSpec((None, None, page, width), lambda b, j, pt: (0, pt[b, j * pages + p], 0, 0))

    grid_spec = pltpu.PrefetchScalarGridSpec(
        num_scalar_prefetch=1,
        grid=(nb, steps),
        in_specs=([pl.BlockSpec((1, heads, dq), lambda b, j, pt: (b, 0, 0)),
                   pl.BlockSpec((1, 1, kvr), lambda b, j, pt: (b, 0, 0)),
                   pl.BlockSpec((1, 1, rope), lambda b, j, pt: (b, 0, 0))]
                  + [page_spec(kvr, p) for p in range(pages)]
                  + [page_spec(rope, p) for p in range(pages)]),
        out_specs=pl.BlockSpec((1, heads, kvr), lambda b, j, pt: (b, 0, 0)),
        scratch_shapes=[pltpu.VMEM((heads, 1), F32), pltpu.VMEM((heads, 1), F32), pltpu.VMEM((heads, kvr), F32)],
    )
    return pl.pallas_call(
        functools.partial(_mla_decode_kernel, pages=pages, kvr=kvr),
        grid_spec=grid_spec,
        out_shape=jax.ShapeDtypeStruct((nb, heads, kvr), F32),
        compiler_params=_params("parallel", "arbitrary"),
        name="mla_decode",
    )(page_table, q, ckv_new, kpe_new, *([cache_ckv] * pages), *([cache_kpe] * pages))


def _gla_decode_kernel(q_ref, k_ref, v_ref, la_ref, s_ref, o_ref, snew_ref, *, heads, dk, dv):
    tb = q_ref.shape[0]
    decay = jnp.exp(la_ref[...])
    q = q_ref[...]
    k = k_ref[...]
    v = v_ref[...].astype(F32)
    qd = (q * decay).astype(BF16)
    head = lambda a, h, n: a[:, h * n:(h + 1) * n]
    rows = jnp.concatenate([head(decay, h, dk) for h in range(heads)] + [head(k, h, dk) for h in range(heads)]
                           + [jnp.zeros((dk - 2 * heads * tb, dk), F32)], axis=0)
    cols = jnp.transpose(rows)
    for bi in range(tb):
        for h in range(heads):
            s_old = s_ref[bi, h]
            qh = head(q, h, dk)[bi:bi + 1]
            kh = head(k, h, dk)[bi:bi + 1]
            vh = head(v, h, dv)[bi:bi + 1]
            qk = jnp.sum(qh * kh, axis=-1, keepdims=True)
            qd8 = jnp.broadcast_to(head(qd, h, dk)[bi:bi + 1], (8, dk))
            o_h = jnp.dot(qd8, s_old.astype(BF16), preferred_element_type=F32)[:1, :] + qk * vh
            o_ref[bi:bi + 1, h * dv:(h + 1) * dv] = o_h
            c_decay = h * tb + bi
            c_key = (heads + h) * tb + bi
            snew_ref[bi, h] = s_old * cols[:, c_decay:c_decay + 1] + cols[:, c_key:c_key + 1] * vh


def _gla_decode(q, k, v, la, state, *, tb):
    nb, heads, dk, dv = state.shape
    row = lambda n: pl.BlockSpec((tb, n), lambda i: (i, 0))
    st = pl.BlockSpec((tb, heads, dk, dv), lambda i: (i, 0, 0, 0))
    return pl.pallas_call(
        functools.partial(_gla_decode_kernel, heads=heads, dk=dk, dv=dv),
        grid=(nb // tb,),
        in_specs=[row(heads * dk), row(heads * dk), row(heads * dv), row(heads * dk), st],
        out_specs=[row(heads * dv), st],
        out_shape=[jax.ShapeDtypeStruct((nb, heads * dv), F32), jax.ShapeDtypeStruct(state.shape, F32)],
        compiler_params=_params("parallel"),
        name="gla_decode",
    )(q, k, v, la, state)


def _mix_out_kernel(x_ref, lng_ref, lnb_ref, og_ref, g_ref, ol_ref, ng_ref, wv_ref, wo_ref, l1g_ref, l1b_ref,
                    h1_ref, mix_sc, *, alpha):
    gla_heads, _, dv = og_ref.shape
    mla_heads, _, vd = wv_ref.shape[0], wv_ref.shape[1], wv_ref.shape[2]
    gw = gla_heads * dv
    gate = g_ref[...]
    for h in range(gla_heads):
        og = _rms_norm(og_ref[h], ng_ref[...])
        gh = gate[:, h * dv:(h + 1) * dv]
        mix_sc[:, h * dv:(h + 1) * dv] = (og * (gh * jax.nn.sigmoid(gh))).astype(BF16)
    for h in range(mla_heads):
        om = jnp.dot(ol_ref[h], wv_ref[h], preferred_element_type=F32)
        mix_sc[:, gw + h * vd:gw + (h + 1) * vd] = om.astype(BF16)
    mix = jnp.dot(mix_sc[...], wo_ref[...], preferred_element_type=F32)
    hres = _layer_norm(x_ref[...], lng_ref[...], lnb_ref[...])
    h1_ref[...] = _layer_norm(alpha * hres + mix, l1g_ref[...], l1b_ref[...])


def _mix_out(x, lng, lnb, og, gate, ol, ng, wv, wo, l1g, l1b, *, tm, alpha):
    m, d = x.shape
    gla_heads, _, dv = og.shape
    mla_heads, _, kvr = ol.shape
    row = lambda n: pl.BlockSpec((tm, n), lambda i: (i, 0))
    return pl.pallas_call(
        functools.partial(_mix_out_kernel, alpha=alpha),
        grid=(m // tm,),
        in_specs=[row(d), _full(lng.shape), _full(lnb.shape),
                  pl.BlockSpec((gla_heads, tm, dv), lambda i: (0, i, 0)), row(gate.shape[1]),
                  pl.BlockSpec((mla_heads, tm, kvr), lambda i: (0, i, 0)),
                  _full(ng.shape), _full(wv.shape), _full(wo.shape), _full(l1g.shape), _full(l1b.shape)],
        out_specs=row(d),
        out_shape=jax.ShapeDtypeStruct((m, d), F32),
        scratch_shapes=[pltpu.VMEM((tm, wo.shape[0]), BF16)],
        compiler_params=_params("parallel"),
        name="mix_out",
    )(x, lng, lnb, og, gate, ol, ng, wv, wo, l1g, l1b)


def _mlp_kernel(h_ref, wu_ref, wd_ref, l2g_ref, l2b_ref, o_ref, hb_sc, *, alpha):
    j = pl.program_id(1)

    @pl.when(j == 0)
    def _():
        hb_sc[...] = h_ref[...].astype(BF16)
        o_ref[...] = jnp.zeros_like(o_ref)

    u = jnp.maximum(jnp.dot(hb_sc[...], wu_ref[...], preferred_element_type=F32), 0.0)
    o_ref[...] += jnp.dot((u * u).astype(BF16), wd_ref[...], preferred_element_type=F32)

    @pl.when(j == pl.num_programs(1) - 1)
    def _():
        o_ref[...] = _layer_norm(alpha * h_ref[...] + o_ref[...], l2g_ref[...], l2b_ref[...])


def _mlp(h1, wu, wd, l2g, l2b, *, tm, tf, alpha):
    m, d = h1.shape
    ff = wu.shape[1]
    return pl.pallas_call(
        functools.partial(_mlp_kernel, alpha=alpha),
        grid=(m // tm, ff // tf),
        in_specs=[pl.BlockSpec((tm, d), lambda i, j: (i, 0)), pl.BlockSpec((d, tf), lambda i, j: (0, j)),
                  pl.BlockSpec((tf, d), lambda i, j: (j, 0)), _full(l2g.shape), _full(l2b.shape)],
        out_specs=pl.BlockSpec((tm, d), lambda i, j: (i, 0)),
        out_shape=jax.ShapeDtypeStruct((m, d), F32),
        scratch_shapes=[pltpu.VMEM((tm, d), BF16)],
        compiler_params=_params("parallel", "arbitrary"),
        name="mlp",
    )(h1, wu, wd, l2g, l2b)


def _rope_tables(pos, rope, heads):
    half = rope // 2
    inv_freq = ROPE_BASE ** (-jnp.arange(half, dtype=F32) / half)
    ang = pos.astype(F32)[:, None] * inv_freq[None, :]
    cos, sin = jnp.cos(ang), jnp.sin(ang)
    return jnp.tile(jnp.concatenate([cos, cos], axis=1), (1, heads)), jnp.tile(jnp.concatenate([sin, sin], axis=1), (1, heads))


def _rotate_half_cols(w):
    half = w.shape[-1] // 2
    return jnp.concatenate([-w[..., half:], w[..., :half]], axis=-1)


def _row_tile(m, cap):
    t = min(m, cap)
    while m % t:
        t //= 2
    return t


def kernel(x_prompt, x_sample, cache_ckv, cache_kpe, state_gla, page_table, meta_tokens, ln_emb_g, ln_emb_b, w_in, w_gla_a2, b_gla_a, gla_norm_g, mla_q_norm_g, w_mla_uq, mla_kv_norm_g, w_mla_ukv, w_out, ln1_g, ln1_b, w_up, w_down, ln2_g, ln2_b):
    bsz, seq, d = x_prompt.shape
    nb, dec_seq, _ = x_sample.shape
    depth = w_in.shape[0]
    assert depth == 1 and dec_seq == 1
    _, gla_heads, dk, dv = state_gla.shape[1:]
    page, kvr = cache_ckv.shape[2:]
    rope = cache_kpe.shape[3]
    n_pages = page_table.shape[1]
    n_meta = meta_tokens.shape[0]
    rank = w_gla_a2.shape[1]
    kw, gw = gla_heads * dk, gla_heads * dv
    qr = mla_q_norm_g.shape[-1]
    mla_heads = w_mla_ukv.shape[2]
    nope = w_mla_uq.shape[-1] // mla_heads - rope
    alpha = (2 * depth) ** 0.25
    mla_scale = (nope + rope) ** -0.5
    assert rank <= LANES and seq % GLA_CHUNK == 0 and n_pages % DECODE_PAGES_PER_STEP == 0

    w0 = w_in[0]
    o_ga = 2 * kw + 2 * gw
    o_cq = o_ga + rank
    o_kpe = o_cq + qr + kvr
    w_gla = jnp.concatenate([w0[:, :o_ga], jnp.pad(w0[:, o_ga:o_cq], ((0, 0), (0, LANES - rank)))], axis=1).astype(BF16)
    w_a2 = jnp.pad(w_gla_a2[0], ((0, LANES - rank), (0, 0))).astype(BF16)
    b_a = b_gla_a[0][None, :]
    w_kpe = w0[:, o_kpe:]
    w_mla = jnp.concatenate([w0[:, o_cq:], _rotate_half_cols(w_kpe)], axis=1).astype(BF16)
    uq = w_mla_uq[0].reshape(qr, mla_heads, nope + rope)
    uq_rope = uq[..., nope:]
    w_uq = jnp.concatenate([uq[..., :nope].reshape(qr, mla_heads * nope), uq_rope.reshape(qr, mla_heads * rope),
                            _rotate_half_cols(uq_rope).reshape(qr, mla_heads * rope)], axis=1).astype(BF16)
    ukv = w_mla_ukv[0]
    w_n = jnp.transpose(ukv[..., :nope], (1, 2, 0)).astype(BF16)
    w_v = jnp.transpose(ukv[..., nope:], (1, 0, 2)).astype(BF16)
    w_o = w_out[0].astype(BF16)
    w_u = w_up[0].astype(BF16)
    w_d = w_down[0].astype(BF16)
    lng, lnb = ln_emb_g[None, :], ln_emb_b[None, :]
    qg, kvg, ng = mla_q_norm_g, mla_kv_norm_g, gla_norm_g
    l1g, l1b, l2g, l2b = ln1_g, ln1_b, ln2_g, ln2_b

    cos_m, sin_m = _rope_tables(jnp.arange(n_meta, dtype=jnp.int32), rope, mla_heads)
    cos_p, sin_p = _rope_tables(n_meta + jnp.arange(seq, dtype=jnp.int32), rope, mla_heads)
    cos_s, sin_s = _rope_tables(jnp.full((nb,), n_pages * page, dtype=jnp.int32), rope, mla_heads)

    def project(x2d, cos8, sin8, tm):
        gla = _proj_gla(x2d, lng, lnb, w_gla, w_a2, b_a, tm=tm, kw=kw, gw=gw, q_scale=dk ** -0.5)
        mla = _proj_mla(x2d, lng, lnb, w_mla, qg, kvg, w_uq, w_n, cos8, sin8, tm=tm, scale=mla_scale)
        return gla, mla

    def finish(x2d, o_gla, gate, o_lat, tm, tm_mlp):
        h1 = _mix_out(x2d, lng, lnb, o_gla, gate, o_lat, ng, w_v, w_o, l1g, l1b, tm=tm, alpha=alpha)
        return _mlp(h1, w_u, w_d, l2g, l2b, tm=tm_mlp, tf=512, alpha=alpha)

    xp = x_prompt.reshape(bsz * seq, d)
    (_, km, vm, _, lam), (_, ckv_m, kpe_m, kcat_m) = project(meta_tokens, cos_m, sin_m, n_meta)
    tm_p = _row_tile(seq, 256)
    (qg_p, kg_p, vg_p, gate_p, la_p), (q_p, ckv_p, kpe_p, kcat_p) = project(xp, cos_p, sin_p, tm_p)
    o_gla_p, s_p = _gla_prompt(qg_p, kg_p, vg_p, la_p, km, vm, lam,
                               bsz=bsz, seq=seq, heads=gla_heads, dk=dk, dv=dv)
    o_lat_p = _mla_prompt(q_p, kcat_p, kcat_m, bsz=bsz, seq=seq, tq=_row_tile(seq, 256), kvr=kvr)
    y_p = finish(xp, o_gla_p, gate_p, o_lat_p, tm_p, _row_tile(bsz * seq, 1024))

    xs = x_sample.reshape(nb, d)
    (qg_s, kg_s, vg_s, gate_s, la_s), (q_s, ckv_s, kpe_s, _) = project(xs, cos_s, sin_s, nb)
    o_gla_s, s_s = _gla_decode(qg_s, kg_s, vg_s, la_s, state_gla[0], tb=_row_tile(nb, 8))
    q_dec = jnp.transpose(q_s, (1, 0, 2)).astype(F32)
    o_lat_s = _mla_decode(page_table, q_dec, ckv_s[:, None, :], kpe_s[:, None, :], cache_ckv, cache_kpe,
                          pages=DECODE_PAGES_PER_STEP)
    o_gla_s = jnp.transpose(o_gla_s.reshape(nb, gla_heads, dv), (1, 0, 2))
    o_lat_s = jnp.transpose(o_lat_s, (1, 0, 2)).astype(BF16)
    y_s = finish(xs, o_gla_s, gate_s, o_lat_s, nb, nb)

    def with_meta(meta_rows, rows):
        meta_b = jnp.broadcast_to(meta_rows[None], (bsz,) + meta_rows.shape)
        return jnp.concatenate([meta_b, rows.reshape(bsz, seq, rows.shape[-1])], axis=1)[None]

    return (y_p.reshape(bsz, seq, d), y_s.reshape(nb, 1, d),
            with_meta(ckv_m, ckv_p), with_meta(kpe_m, kpe_p), s_p[None],
            ckv_s.reshape(1, nb, 1, kvr), kpe_s.reshape(1, nb, 1, rope), s_s[None])
```

```python
import functools

import jax
import jax.numpy as jnp
from jax import lax
from jax.experimental import pallas as pl
from jax.experimental.pallas import tpu as pltpu

F32 = jnp.float32
BF16 = jnp.bfloat16

LN_EPS = 1e-5
RMS_EPS = 1e-6
ROPE_BASE = 10000.0
GLA_GATE_TAU = 16.0
GLA_CHUNK = 64
GLA_SAFE_RANGE = 60.0
GLA_UNROLL = 4
LANES = 128
VMEM_LIMIT_BYTES = 56 * 1024 * 1024
DECODE_GROUP_PAGES = 32
MLA_PROMPT_BLOCK = 512
MLA_PROMPT_HEADS_PER_BLOCK = 4


def _params(*sem):
    return pltpu.CompilerParams(dimension_semantics=sem, vmem_limit_bytes=VMEM_LIMIT_BYTES)


def _layer_norm(x, g, b):
    mu = jnp.mean(x, axis=-1, keepdims=True)
    xc = x - mu
    var = jnp.mean(xc * xc, axis=-1, keepdims=True)
    return xc * lax.rsqrt(var + LN_EPS) * g + b


def _rms_norm(x, g):
    return x * lax.rsqrt(jnp.mean(x * x, axis=-1, keepdims=True) + RMS_EPS) * g


def _full(shape):
    return pl.BlockSpec(shape, lambda *_: (0,) * len(shape))


def _proj_gla_kernel(x_ref, lng_ref, lnb_ref, w_ref, wa2_ref, ba_ref,
                     q_ref, k_ref, v_ref, g_ref, la_ref, *, kw, gw, q_scale):
    h = _layer_norm(x_ref[...], lng_ref[...], lnb_ref[...]).astype(BF16)
    z = jnp.dot(h, w_ref[...], preferred_element_type=F32)
    q_ref[...] = z[:, :kw] * q_scale
    k_ref[...] = z[:, kw:2 * kw]
    v_ref[...] = z[:, 2 * kw:2 * kw + gw].astype(BF16)
    g_ref[...] = z[:, 2 * kw + gw:2 * kw + 2 * gw]
    ga = z[:, 2 * kw + 2 * gw:].astype(BF16)
    a = jnp.dot(ga, wa2_ref[...], preferred_element_type=F32) + ba_ref[...]
    la_ref[...] = jax.nn.log_sigmoid(a) * (1.0 / GLA_GATE_TAU)


def _proj_gla(x, lng, lnb, w, wa2, ba, *, tm, kw, gw, q_scale):
    m, d = x.shape
    row = lambda n: pl.BlockSpec((tm, n), lambda i: (i, 0))
    return pl.pallas_call(
        functools.partial(_proj_gla_kernel, kw=kw, gw=gw, q_scale=q_scale),
        grid=(m // tm,),
        in_specs=[row(d), _full(lng.shape), _full(lnb.shape), _full(w.shape), _full(wa2.shape), _full(ba.shape)],
        out_specs=[row(kw), row(kw), row(gw), row(gw), row(kw)],
        out_shape=[jax.ShapeDtypeStruct((m, kw), F32), jax.ShapeDtypeStruct((m, kw), F32),
                   jax.ShapeDtypeStruct((m, gw), BF16), jax.ShapeDtypeStruct((m, gw), F32),
                   jax.ShapeDtypeStruct((m, kw), F32)],
        compiler_params=_params("parallel"),
        name="proj_gla",
    )(x, lng, lnb, w, wa2, ba)


def _proj_mla_kernel(x_ref, lng_ref, lnb_ref, w_ref, qg_ref, kvg_ref, wuq_ref, wn_ref, cos_ref, sin_ref,
                     q_ref, ckv_ref, kpe_ref, kcat_ref, *, qr, kvr, rope, heads, nope, scale):
    h = _layer_norm(x_ref[...], lng_ref[...], lnb_ref[...]).astype(BF16)
    z = jnp.dot(h, w_ref[...], preferred_element_type=F32)
    cqn = _rms_norm(z[:, :qr], qg_ref[...]).astype(BF16)
    ckvn = _rms_norm(z[:, qr:qr + kvr], kvg_ref[...])
    cos = cos_ref[...]
    sin = sin_ref[...]
    kpe = z[:, qr + kvr:qr + kvr + rope]
    kpe_rot = z[:, qr + kvr + rope:]
    kpe_r = kpe * cos[:, :rope] + kpe_rot * sin[:, :rope]
    ckv_ref[...] = ckvn
    kpe_ref[...] = kpe_r
    kcat_ref[:, :kvr] = ckvn.astype(BF16)
    kcat_ref[:, kvr:] = kpe_r.astype(BF16)
    qf = jnp.dot(cqn, wuq_ref[...], preferred_element_type=F32)
    hn = heads * nope
    hr = heads * rope
    q_pe = (qf[:, hn:hn + hr] * cos + qf[:, hn + hr:] * sin) * scale
    for hd in range(heads):
        q_lat = jnp.dot(qf[:, hd * nope:(hd + 1) * nope].astype(BF16), wn_ref[hd],
                        preferred_element_type=F32) * scale
        q_ref[hd, :, :kvr] = q_lat.astype(BF16)
        q_ref[hd, :, kvr:] = q_pe[:, hd * rope:(hd + 1) * rope].astype(BF16)


def _proj_mla(x, lng, lnb, w, qg, kvg, wuq, wn, cos8, sin8, *, tm, scale):
    m, d = x.shape
    heads, nope, kvr = wn.shape
    qr = qg.shape[-1]
    rope = cos8.shape[-1] // heads
    n_pos_blocks = cos8.shape[0] // tm
    row = lambda n: pl.BlockSpec((tm, n), lambda i: (i, 0))
    tab = pl.BlockSpec((tm, heads * rope), lambda i: (i % n_pos_blocks, 0))
    return pl.pallas_call(
        functools.partial(_proj_mla_kernel, qr=qr, kvr=kvr, rope=rope, heads=heads, nope=nope, scale=scale),
        grid=(m // tm,),
        in_specs=[row(d), _full(lng.shape), _full(lnb.shape), _full(w.shape), _full(qg.shape), _full(kvg.shape),
                  _full(wuq.shape), _full(wn.shape), tab, tab],
        out_specs=[pl.BlockSpec((heads, tm, kvr + rope), lambda i: (0, i, 0)), row(kvr), row(rope), row(kvr + rope)],
        out_shape=[jax.ShapeDtypeStruct((heads, m, kvr + rope), BF16), jax.ShapeDtypeStruct((m, kvr), F32),
                   jax.ShapeDtypeStruct((m, rope), F32), jax.ShapeDtypeStruct((m, kvr + rope), BF16)],
        compiler_params=_params("parallel"),
        name="proj_mla",
    )(x, lng, lnb, w, qg, kvg, wuq, wn, cos8, sin8)


def _mla_prompt_kernel(q_ref, k_ref, kmeta_ref, o_ref, m_sc, l_sc, acc_sc, *, tq, kvr):
    i = pl.program_id(2)
    heads = q_ref.shape[0]
    rows = heads * tq
    q = q_ref[...].reshape(rows, q_ref.shape[2])
    nt = (((1,), (1,)), ((), ()))

    km = kmeta_ref[...]
    s = lax.dot_general(q, km, nt, preferred_element_type=F32)
    m0 = jnp.max(s, axis=-1, keepdims=True)
    p = jnp.exp(s - m0)
    m_sc[...] = jnp.broadcast_to(m0, (rows, LANES))
    lane = lax.broadcasted_iota(jnp.int32, (rows, LANES), 1)
    l_sc[...] = jnp.where(lane == 0, jnp.sum(p, axis=-1, keepdims=True), 0.0)
    acc_sc[...] = jnp.dot(p.astype(BF16), km[:, :kvr], preferred_element_type=F32)

    def update(s, kblk):
        chunks = [s[:, c * LANES:(c + 1) * LANES] for c in range(s.shape[1] // LANES)]
        m_prev = m_sc[...]
        row_max = jnp.max(functools.reduce(jnp.maximum, chunks), axis=-1, keepdims=True)
        m_new = jnp.maximum(m_prev, jnp.broadcast_to(row_max, (rows, LANES)))
        corr = jnp.exp(m_prev - m_new)
        ps = [jnp.exp(c - m_new) for c in chunks]
        l_sc[...] = l_sc[...] * corr + functools.reduce(jnp.add, ps)
        p = jnp.concatenate([x.astype(BF16) for x in ps], axis=1)
        pv = jnp.dot(p, kblk[:, :kvr], preferred_element_type=F32)
        acc_sc[...] = acc_sc[...] * jnp.concatenate([corr] * (kvr // LANES), axis=1) + pv
        m_sc[...] = m_new

    def body(j, carry):
        kblk = k_ref[pl.ds(pl.multiple_of(j * tq, tq), tq), :]
        update(lax.dot_general(q, kblk, nt, preferred_element_type=F32), kblk)
        return carry

    lax.fori_loop(0, i, body, 0)

    kblk = k_ref[pl.ds(pl.multiple_of(i * tq, tq), tq), :]
    s = lax.dot_general(q, kblk, nt, preferred_element_type=F32)
    tok = lax.broadcasted_iota(jnp.int32, (heads, tq, tq), 1).reshape(rows, tq)
    key = lax.broadcasted_iota(jnp.int32, (rows, tq), 1)
    update(jnp.where(key <= tok, s, -jnp.inf), kblk)

    o = acc_sc[...] / jnp.sum(l_sc[...], axis=-1, keepdims=True)
    o_ref[...] = o.reshape(heads, tq, kvr).astype(o_ref.dtype)


def _mla_prompt(q, kcat, kmeta, *, bsz, seq, tq, kvr, heads_per_block):
    heads, m, dq = q.shape
    nq = seq // tq
    rows = heads_per_block * tq
    return pl.pallas_call(
        functools.partial(_mla_prompt_kernel, tq=tq, kvr=kvr),
        grid=(bsz, heads // heads_per_block, nq),
        in_specs=[pl.BlockSpec((heads_per_block, tq, dq), lambda b, h, i: (h, b * nq + i, 0)),
                  pl.BlockSpec((seq, dq), lambda b, h, i: (b, 0)),
                  pl.BlockSpec(kmeta.shape, lambda b, h, i: (0, 0))],
        out_specs=pl.BlockSpec((heads_per_block, tq, kvr), lambda b, h, i: (h, b * nq + i, 0)),
        out_shape=jax.ShapeDtypeStruct((heads, m, kvr), BF16),
        scratch_shapes=[pltpu.VMEM((rows, LANES), F32), pltpu.VMEM((rows, LANES), F32), pltpu.VMEM((rows, kvr), F32)],
        compiler_params=_params("parallel", "parallel", "arbitrary"),
        name="mla_prompt",
    )(q, kcat, kmeta)


def _split3(x):
    hi = x.astype(BF16)
    r1 = x - hi.astype(F32)
    mid = r1.astype(BF16)
    lo = (r1 - mid.astype(F32)).astype(BF16)
    return hi, mid, lo


def _cumsum_rows(x):
    c = x.shape[0]
    tri = (lax.broadcasted_iota(jnp.int32, (c, c), 1) <= lax.broadcasted_iota(jnp.int32, (c, c), 0)).astype(BF16)
    hi, mid, lo = _split3(x)
    dot = lambda t: jnp.dot(tri, t, preferred_element_type=F32)
    return dot(hi) + dot(mid) + dot(lo)


def _gla_state_terms(kd, v, decay_last):
    c, dk = kd.shape
    aug = jnp.concatenate([kd, jnp.broadcast_to(decay_last, (8, dk)), jnp.zeros((dk - c - 8, dk), F32)], axis=0)
    aug_t = jnp.transpose(aug)
    return aug_t[:, c:c + 1], jnp.dot(aug_t[:, :c].astype(BF16), v, preferred_element_type=F32)


def _gla_prompt_kernel(q_ref, k_ref, v_ref, la_ref, km_ref, vm_ref, lam_ref, o_ref, sfin_ref, *, chunk):
    seq = q_ref.shape[0]
    n_chunks = seq // chunk

    bm = _cumsum_rows(lam_ref[...])
    bm_last = bm[-1:, :]
    _, s_meta = _gla_state_terms(km_ref[...] * jnp.exp(bm_last - bm), vm_ref[...], jnp.exp(bm_last))

    la_all = la_ref[...]
    chunk_decay = -jnp.sum(la_all.reshape(n_chunks, chunk, la_all.shape[1]), axis=1)
    safe = jnp.max(chunk_decay) <= GLA_SAFE_RANGE

    def load(c):
        r0 = pl.multiple_of(c * chunk, chunk)
        rows = pl.ds(r0, chunk)
        return rows, q_ref[rows, :], k_ref[rows, :], v_ref[rows, :], la_ref[rows, :]

    causal = lax.broadcasted_iota(jnp.int32, (chunk, chunk), 1) <= lax.broadcasted_iota(jnp.int32, (chunk, chunk), 0)

    def fast_chunk(c, s):
        rows, q, k, v, la = load(c)
        b = _cumsum_rows(la)
        b_last = b[-1:, :]
        o_inter = jnp.dot((q * jnp.exp(b)).astype(BF16), s.astype(BF16), preferred_element_type=F32)
        kd = k * jnp.exp(b_last - b)
        qd = q * jnp.exp(b - b_last)
        a = lax.dot_general(qd.astype(BF16), kd.astype(BF16), (((1,), (1,)), ((), ())), preferred_element_type=F32)
        a = jnp.where(causal, a, 0.0)
        o_ref[rows, :] = o_inter + jnp.dot(a.astype(BF16), v, preferred_element_type=F32)
        decay_col, kv = _gla_state_terms(kd, v, jnp.exp(b_last))
        return s * decay_col + kv

    def exact_chunk(c, s):
        rows, q, k, v, la = load(c)
        r0 = pl.multiple_of(c * chunk, chunk)
        b = _cumsum_rows(la)
        b_last = b[-1:, :]
        o_ref[rows, :] = jnp.dot((q * jnp.exp(b)).astype(BF16), s.astype(BF16), preferred_element_type=F32)
        vf = v.astype(F32)
        row_id = lax.broadcasted_iota(jnp.int32, (chunk, 1), 0)

        def token(t, carry2):
            sel = row_id == t
            bt = jnp.sum(jnp.where(sel, b, 0.0), axis=0, keepdims=True)
            qt = jnp.sum(jnp.where(sel, q, 0.0), axis=0, keepdims=True)
            w = jnp.exp(jnp.where(row_id <= t, bt - b, -jnp.inf))
            a_col = jnp.sum(w * qt * k, axis=-1, keepdims=True)
            o_t = jnp.sum(a_col * vf, axis=0, keepdims=True)
            o_ref[pl.ds(r0 + t, 1), :] = o_ref[pl.ds(r0 + t, 1), :] + o_t
            return carry2

        lax.fori_loop(0, chunk, token, 0)
        decay_col, kv = _gla_state_terms(k * jnp.exp(b_last - b), v, jnp.exp(b_last))
        return s * decay_col + kv

    @pl.when(safe)
    def _():
        sfin_ref[...] = lax.fori_loop(0, n_chunks, fast_chunk, s_meta, unroll=GLA_UNROLL)

    @pl.when(jnp.logical_not(safe))
    def _():
        sfin_ref[...] = lax.fori_loop(0, n_chunks, exact_chunk, s_meta)


def _gla_prompt(q, k, v, la, km, vm, lam, *, bsz, seq, heads, dk, dv):
    m = q.shape[0]
    nm = km.shape[0]
    col = lambda n: pl.BlockSpec((seq, n), lambda b, h: (b, h))
    mcol = lambda n: pl.BlockSpec((nm, n), lambda b, h: (0, h))
    return pl.pallas_call(
        functools.partial(_gla_prompt_kernel, chunk=GLA_CHUNK),
        grid=(bsz, heads),
        in_specs=[col(dk), col(dk), col(dv), col(dk), mcol(dk), mcol(dv), mcol(dk)],
        out_specs=[pl.BlockSpec((None, seq, dv), lambda b, h: (h, b, 0)),
                   pl.BlockSpec((None, None, dk, dv), lambda b, h: (b, h, 0, 0))],
        out_shape=[jax.ShapeDtypeStruct((heads, m, dv), F32), jax.ShapeDtypeStruct((bsz, heads, dk, dv), F32)],
        compiler_params=_params("parallel", "parallel"),
        name="gla_prompt",
    )(q, k, v, la, km, vm, lam)


def _mla_decode_kernel(pt_ref, q_ref, ckvs_ref, kpes_ref, ckv_hbm, kpe_hbm, o_ref,
                       ckv_buf, kpe_buf, ckv_sem, kpe_sem, *, group, kvr, page):
    nb, n_pages = pt_ref.shape
    steps = n_pages // group
    heads = q_ref.shape[1]

    def copies(b, g, slot):
        out = []
        for p in range(group):
            pid = pt_ref[b, g * group + p]
            out.append(pltpu.make_async_copy(ckv_hbm.at[0, pid], ckv_buf.at[slot, pl.ds(p * page, page), :],
                                             ckv_sem.at[slot]))
            out.append(pltpu.make_async_copy(kpe_hbm.at[0, pid], kpe_buf.at[slot, :, pl.ds(p * page, page)],
                                             kpe_sem.at[slot]))
        return out

    def start(b, g, slot):
        for c in copies(b, g, slot):
            c.start()

    def wait(slot):
        for c in copies(0, 0, slot):
            c.wait()

    b = pl.program_id(0)

    @pl.when(b == 0)
    def _():
        start(0, 0, 0)

    nt = (((1,), (1,)), ((), ()))
    q = q_ref[b]
    ckv_new = ckvs_ref[b]
    kpe_new = kpes_ref[b]
    m = (jnp.sum(q[:, :kvr] * ckv_new, axis=-1, keepdims=True)
         + jnp.sum(q[:, kvr:] * kpe_new, axis=-1, keepdims=True))
    l = jnp.ones_like(m)
    acc = jnp.broadcast_to(ckv_new, (heads, kvr))
    q_lat = q[:, :kvr].astype(BF16)
    q_pe = q[:, kvr:].astype(BF16)
    for g in range(steps):
        slot = g % 2
        wait(slot)
        if g + 1 < steps:
            start(b, g + 1, 1 - slot)
        else:
            @pl.when(b + 1 < nb)
            def _():
                start(b + 1, 0, 1 - slot)
        ck = ckv_buf[slot].astype(BF16)
        kp = kpe_buf[slot].astype(BF16)
        s = (lax.dot_general(q_lat, ck, nt, preferred_element_type=F32)
             + jnp.dot(q_pe, kp, preferred_element_type=F32))
        m_new = jnp.maximum(m, jnp.max(s, axis=-1, keepdims=True))
        corr = jnp.exp(m - m_new)
        p = jnp.exp(s - m_new)
        l = l * corr + jnp.sum(p, axis=-1, keepdims=True)
        acc = acc * corr + jnp.dot(p.astype(BF16), ck, preferred_element_type=F32)
        m = m_new
    o_ref[b] = acc / l


def _decode_group(n_pages):
    group = min(DECODE_GROUP_PAGES, n_pages // 2)
    while n_pages % (2 * group):
        group -= 1
    return group


def _mla_decode(page_table, q, ckv_new, kpe_new, cache_ckv, cache_kpe_t):
    nb, heads, _ = q.shape
    n_pages = page_table.shape[1]
    page, kvr = cache_ckv.shape[2:]
    rope = cache_kpe_t.shape[2]
    group = _decode_group(n_pages)
    vmem = pl.BlockSpec(memory_space=pltpu.VMEM)
    hbm = pl.BlockSpec(memory_space=pl.ANY)
    grid_spec = pltpu.PrefetchScalarGridSpec(
        num_scalar_prefetch=1,
        grid=(nb,),
        in_specs=[vmem, vmem, vmem, hbm, hbm],
        out_specs=vmem,
        scratch_shapes=[pltpu.VMEM((2, group * page, kvr), F32), pltpu.VMEM((2, rope, group * page), F32),
                        pltpu.SemaphoreType.DMA((2,)), pltpu.SemaphoreType.DMA((2,))],
    )
    return pl.pallas_call(
        functools.partial(_mla_decode_kernel, group=group, kvr=kvr, page=page),
        grid_spec=grid_spec,
        out_shape=jax.ShapeDtypeStruct((nb, heads, kvr), F32),
        compiler_params=_params("arbitrary"),
        name="mla_decode",
    )(page_table, q, ckv_new, kpe_new, cache_ckv, cache_kpe_t)


def _gla_decode_kernel(q_ref, k_ref, v_ref, la_ref, s_ref, o_ref, snew_ref, *, heads, dk, dv):
    tb = q_ref.shape[0]
    decay = jnp.exp(la_ref[...])
    q = q_ref[...]
    k = k_ref[...]
    v = v_ref[...].astype(F32)
    qd = (q * decay).astype(BF16)
    head = lambda a, h, n: a[:, h * n:(h + 1) * n]
    rows = jnp.concatenate([head(decay, h, dk) for h in range(heads)] + [head(k, h, dk) for h in range(heads)]
                           + [jnp.zeros((dk - 2 * heads * tb, dk), F32)], axis=0)
    cols = jnp.transpose(rows)
    for bi in range(tb):
        for h in range(heads):
            s_old = s_ref[bi, h]
            qh = head(q, h, dk)[bi:bi + 1]
            kh = head(k, h, dk)[bi:bi + 1]
            vh = head(v, h, dv)[bi:bi + 1]
            qk = jnp.sum(qh * kh, axis=-1, keepdims=True)
            qd8 = jnp.broadcast_to(head(qd, h, dk)[bi:bi + 1], (8, dk))
            o_h = jnp.dot(qd8, s_old.astype(BF16), preferred_element_type=F32)[:1, :] + qk * vh
            o_ref[bi:bi + 1, h * dv:(h + 1) * dv] = o_h
            c_decay = h * tb + bi
            c_key = (heads + h) * tb + bi
            snew_ref[bi, h] = s_old * cols[:, c_decay:c_decay + 1] + cols[:, c_key:c_key + 1] * vh


def _gla_decode(q, k, v, la, state, *, tb):
    nb, heads, dk, dv = state.shape
    row = lambda n: pl.BlockSpec((tb, n), lambda i: (i, 0))
    st = pl.BlockSpec((tb, heads, dk, dv), lambda i: (i, 0, 0, 0))
    return pl.pallas_call(
        functools.partial(_gla_decode_kernel, heads=heads, dk=dk, dv=dv),
        grid=(nb // tb,),
        in_specs=[row(heads * dk), row(heads * dk), row(heads * dv), row(heads * dk), st],
        out_specs=[row(heads * dv), st],
        out_shape=[jax.ShapeDtypeStruct((nb, heads * dv), F32), jax.ShapeDtypeStruct(state.shape, F32)],
        compiler_params=_params("parallel"),
        name="gla_decode",
    )(q, k, v, la, state)


def _mix_out_kernel(x_ref, lng_ref, lnb_ref, og_ref, g_ref, ol_ref, ng_ref, wv_ref, wo_ref, l1g_ref, l1b_ref,
                    h1_ref, mix_sc, *, alpha):
    gla_heads, _, dv = og_ref.shape
    mla_heads, _, vd = wv_ref.shape[0], wv_ref.shape[1], wv_ref.shape[2]
    gw = gla_heads * dv
    gate = g_ref[...]
    for h in range(gla_heads):
        og = _rms_norm(og_ref[h], ng_ref[...])
        gh = gate[:, h * dv:(h + 1) * dv]
        mix_sc[:, h * dv:(h + 1) * dv] = (og * (gh * jax.nn.sigmoid(gh))).astype(BF16)
    for h in range(mla_heads):
        om = jnp.dot(ol_ref[h], wv_ref[h], preferred_element_type=F32)
        mix_sc[:, gw + h * vd:gw + (h + 1) * vd] = om.astype(BF16)
    mix = jnp.dot(mix_sc[...], wo_ref[...], preferred_element_type=F32)
    hres = _layer_norm(x_ref[...], lng_ref[...], lnb_ref[...])
    h1_ref[...] = _layer_norm(alpha * hres + mix, l1g_ref[...], l1b_ref[...])


def _mix_out(x, lng, lnb, og, gate, ol, ng, wv, wo, l1g, l1b, *, tm, alpha):
    m, d = x.shape
    gla_heads, _, dv = og.shape
    mla_heads, _, kvr = ol.shape
    row = lambda n: pl.BlockSpec((tm, n), lambda i: (i, 0))
    return pl.pallas_call(
        functools.partial(_mix_out_kernel, alpha=alpha),
        grid=(m // tm,),
        in_specs=[row(d), _full(lng.shape), _full(lnb.shape),
                  pl.BlockSpec((gla_heads, tm, dv), lambda i: (0, i, 0)), row(gate.shape[1]),
                  pl.BlockSpec((mla_heads, tm, kvr), lambda i: (0, i, 0)),
                  _full(ng.shape), _full(wv.shape), _full(wo.shape), _full(l1g.shape), _full(l1b.shape)],
        out_specs=row(d),
        out_shape=jax.ShapeDtypeStruct((m, d), F32),
        scratch_shapes=[pltpu.VMEM((tm, wo.shape[0]), BF16)],
        compiler_params=_params("parallel"),
        name="mix_out",
    )(x, lng, lnb, og, gate, ol, ng, wv, wo, l1g, l1b)


def _mlp_kernel(h_ref, wu_ref, wd_ref, l2g_ref, l2b_ref, o_ref, hb_sc, *, alpha):
    j = pl.program_id(1)

    @pl.when(j == 0)
    def _():
        hb_sc[...] = h_ref[...].astype(BF16)
        o_ref[...] = jnp.zeros_like(o_ref)

    u = jnp.maximum(jnp.dot(hb_sc[...], wu_ref[...], preferred_element_type=F32), 0.0)
    o_ref[...] += jnp.dot((u * u).astype(BF16), wd_ref[...], preferred_element_type=F32)

    @pl.when(j == pl.num_programs(1) - 1)
    def _():
        o_ref[...] = _layer_norm(alpha * h_ref[...] + o_ref[...], l2g_ref[...], l2b_ref[...])


def _mlp(h1, wu, wd, l2g, l2b, *, tm, tf, alpha):
    m, d = h1.shape
    ff = wu.shape[1]
    return pl.pallas_call(
        functools.partial(_mlp_kernel, alpha=alpha),
        grid=(m // tm, ff // tf),
        in_specs=[pl.BlockSpec((tm, d), lambda i, j: (i, 0)), pl.BlockSpec((d, tf), lambda i, j: (0, j)),
                  pl.BlockSpec((tf, d), lambda i, j: (j, 0)), _full(l2g.shape), _full(l2b.shape)],
        out_specs=pl.BlockSpec((tm, d), lambda i, j: (i, 0)),
        out_shape=jax.ShapeDtypeStruct((m, d), F32),
        scratch_shapes=[pltpu.VMEM((tm, d), BF16)],
        compiler_params=_params("parallel", "arbitrary"),
        name="mlp",
    )(h1, wu, wd, l2g, l2b)


def _rope_tables(pos, rope, heads):
    half = rope // 2
    inv_freq = ROPE_BASE ** (-jnp.arange(half, dtype=F32) / half)
    ang = pos.astype(F32)[:, None] * inv_freq[None, :]
    cos, sin = jnp.cos(ang), jnp.sin(ang)
    return jnp.tile(jnp.concatenate([cos, cos], axis=1), (1, heads)), jnp.tile(jnp.concatenate([sin, sin], axis=1), (1, heads))


def _rotate_half_cols(w):
    half = w.shape[-1] // 2
    return jnp.concatenate([-w[..., half:], w[..., :half]], axis=-1)


def _row_tile(m, cap):
    t = min(m, cap)
    while m % t:
        t //= 2
    return t


def kernel(x_prompt, x_sample, cache_ckv, cache_kpe, state_gla, page_table, meta_tokens, ln_emb_g, ln_emb_b, w_in, w_gla_a2, b_gla_a, gla_norm_g, mla_q_norm_g, w_mla_uq, mla_kv_norm_g, w_mla_ukv, w_out, ln1_g, ln1_b, w_up, w_down, ln2_g, ln2_b):
    bsz, seq, d = x_prompt.shape
    nb, dec_seq, _ = x_sample.shape
    depth = w_in.shape[0]
    assert depth == 1 and dec_seq == 1
    _, gla_heads, dk, dv = state_gla.shape[1:]
    page, kvr = cache_ckv.shape[2:]
    rope = cache_kpe.shape[3]
    n_pages = page_table.shape[1]
    n_meta = meta_tokens.shape[0]
    rank = w_gla_a2.shape[1]
    kw, gw = gla_heads * dk, gla_heads * dv
    qr = mla_q_norm_g.shape[-1]
    mla_heads = w_mla_ukv.shape[2]
    nope = w_mla_uq.shape[-1] // mla_heads - rope
    alpha = (2 * depth) ** 0.25
    mla_scale = (nope + rope) ** -0.5
    assert rank <= LANES and seq % GLA_CHUNK == 0 and n_pages % 2 == 0

    w0 = w_in[0]
    o_ga = 2 * kw + 2 * gw
    o_cq = o_ga + rank
    o_kpe = o_cq + qr + kvr
    w_gla = jnp.concatenate([w0[:, :o_ga], jnp.pad(w0[:, o_ga:o_cq], ((0, 0), (0, LANES - rank)))], axis=1).astype(BF16)
    w_a2 = jnp.pad(w_gla_a2[0], ((0, LANES - rank), (0, 0))).astype(BF16)
    b_a = b_gla_a[0][None, :]
    w_kpe = w0[:, o_kpe:]
    w_mla = jnp.concatenate([w0[:, o_cq:], _rotate_half_cols(w_kpe)], axis=1).astype(BF16)
    uq = w_mla_uq[0].reshape(qr, mla_heads, nope + rope)
    uq_rope = uq[..., nope:]
    w_uq = jnp.concatenate([uq[..., :nope].reshape(qr, mla_heads * nope), uq_rope.reshape(qr, mla_heads * rope),
                            _rotate_half_cols(uq_rope).reshape(qr, mla_heads * rope)], axis=1).astype(BF16)
    ukv = w_mla_ukv[0]
    w_n = jnp.transpose(ukv[..., :nope], (1, 2, 0)).astype(BF16)
    w_v = jnp.transpose(ukv[..., nope:], (1, 0, 2)).astype(BF16)
    w_o = w_out[0].astype(BF16)
    w_u = w_up[0].astype(BF16)
    w_d = w_down[0].astype(BF16)
    lng, lnb = ln_emb_g[None, :], ln_emb_b[None, :]
    qg, kvg, ng = mla_q_norm_g, mla_kv_norm_g, gla_norm_g
    l1g, l1b, l2g, l2b = ln1_g, ln1_b, ln2_g, ln2_b

    cos_m, sin_m = _rope_tables(jnp.arange(n_meta, dtype=jnp.int32), rope, mla_heads)
    cos_p, sin_p = _rope_tables(n_meta + jnp.arange(seq, dtype=jnp.int32), rope, mla_heads)
    cos_s, sin_s = _rope_tables(jnp.full((nb,), n_pages * page, dtype=jnp.int32), rope, mla_heads)

    def project(x2d, cos8, sin8, tm):
        gla = _proj_gla(x2d, lng, lnb, w_gla, w_a2, b_a, tm=tm, kw=kw, gw=gw, q_scale=dk ** -0.5)
        mla = _proj_mla(x2d, lng, lnb, w_mla, qg, kvg, w_uq, w_n, cos8, sin8, tm=tm, scale=mla_scale)
        return gla, mla

    def finish(x2d, o_gla, gate, o_lat, tm, tm_mlp):
        h1 = _mix_out(x2d, lng, lnb, o_gla, gate, o_lat, ng, w_v, w_o, l1g, l1b, tm=tm, alpha=alpha)
        return _mlp(h1, w_u, w_d, l2g, l2b, tm=tm_mlp, tf=512, alpha=alpha)

    xp = x_prompt.reshape(bsz * seq, d)
    (_, km, vm, _, lam), (_, ckv_m, kpe_m, kcat_m) = project(meta_tokens, cos_m, sin_m, n_meta)
    tm_p = _row_tile(seq, 256)
    (qg_p, kg_p, vg_p, gate_p, la_p), (q_p, ckv_p, kpe_p, kcat_p) = project(xp, cos_p, sin_p, tm_p)
    o_gla_p, s_p = _gla_prompt(qg_p, kg_p, vg_p, la_p, km, vm, lam,
                               bsz=bsz, seq=seq, heads=gla_heads, dk=dk, dv=dv)
    o_lat_p = _mla_prompt(q_p, kcat_p, kcat_m, bsz=bsz, seq=seq, tq=_row_tile(seq, MLA_PROMPT_BLOCK), kvr=kvr,
                          heads_per_block=min(mla_heads, MLA_PROMPT_HEADS_PER_BLOCK))
    y_p = finish(xp, o_gla_p, gate_p, o_lat_p, tm_p, _row_tile(bsz * seq, 1024))

    xs = x_sample.reshape(nb, d)
    (qg_s, kg_s, vg_s, gate_s, la_s), (q_s, ckv_s, kpe_s, _) = project(xs, cos_s, sin_s, nb)
    o_gla_s, s_s = _gla_decode(qg_s, kg_s, vg_s, la_s, state_gla[0], tb=_row_tile(nb, 8))
    q_dec = jnp.transpose(q_s, (1, 0, 2)).astype(F32)
    cache_kpe_t = jnp.swapaxes(cache_kpe, 2, 3)
    o_lat_s = _mla_decode(page_table, q_dec, ckv_s[:, None, :], kpe_s[:, None, :], cache_ckv, cache_kpe_t)
    o_gla_s = jnp.transpose(o_gla_s.reshape(nb, gla_heads, dv), (1, 0, 2))
    o_lat_s = jnp.transpose(o_lat_s, (1, 0, 2)).astype(BF16)
    y_s = finish(xs, o_gla_s, gate_s, o_lat_s, nb, nb)

    def with_meta(meta_rows, rows):
        meta_b = jnp.broadcast_to(meta_rows[None], (bsz,) + meta_rows.shape)
        return jnp.concatenate([meta_b, rows.reshape(bsz, seq, rows.shape[-1])], axis=1)[None]

    return (y_p.reshape(bsz, seq, d), y_s.reshape(nb, 1, d),
            with_meta(ckv_m, ckv_p), with_meta(kpe_m, kpe_p), s_p[None],
            ckv_s.reshape(1, nb, 1, kvr), kpe_s.reshape(1, nb, 1, rope), s_s[None])
```

```python
import functools

import jax
import jax.numpy as jnp
from jax import lax
from jax.experimental import pallas as pl
from jax.experimental.pallas import tpu as pltpu

F32 = jnp.float32
BF16 = jnp.bfloat16

LN_EPS = 1e-5
RMS_EPS = 1e-6
ROPE_BASE = 10000.0
GLA_GATE_TAU = 16.0
GLA_CHUNK = 64
GLA_SAFE_RANGE = 60.0
GLA_UNROLL = 4
LANES = 128
VMEM_LIMIT_BYTES = 56 * 1024 * 1024
DECODE_GROUP_PAGES = 32
MLA_PROMPT_BLOCK = 512
MLA_PROMPT_HEADS_PER_BLOCK = 4
PROJ_ROW_TILE = 512


def _params(*sem):
    return pltpu.CompilerParams(dimension_semantics=sem, vmem_limit_bytes=VMEM_LIMIT_BYTES)


def _layer_norm(x, g, b):
    mu = jnp.mean(x, axis=-1, keepdims=True)
    xc = x - mu
    var = jnp.mean(xc * xc, axis=-1, keepdims=True)
    return xc * lax.rsqrt(var + LN_EPS) * g + b


def _rms_norm(x, g):
    return x * lax.rsqrt(jnp.mean(x * x, axis=-1, keepdims=True) + RMS_EPS) * g


def _full(shape):
    return pl.BlockSpec(shape, lambda *_: (0,) * len(shape), pipeline_mode=pl.Buffered(1))


def _split3(x):
    hi = x.astype(BF16)
    r1 = x - hi.astype(F32)
    mid = r1.astype(BF16)
    lo = (r1 - mid.astype(F32)).astype(BF16)
    return hi, mid, lo


def _chunk_cumsum(x, chunk):
    rows = x.shape[0]
    sub = min(rows, max(chunk, LANES))
    r = lax.broadcasted_iota(jnp.int32, (sub, sub), 0)
    d = r - lax.broadcasted_iota(jnp.int32, (sub, sub), 1)
    tri = jnp.where((d >= 0) & (d <= (r & (chunk - 1))), 1.0, 0.0).astype(BF16)
    out = []
    for r0 in range(0, rows, sub):
        hi, mid, lo = _split3(x[r0:r0 + sub])
        dot = lambda t: jnp.dot(tri, t, preferred_element_type=F32)
        out.append(dot(hi) + dot(mid) + dot(lo))
    return out[0] if len(out) == 1 else jnp.concatenate(out, axis=0)


def _proj_gla_kernel(x_ref, lng_ref, lnb_ref, w_ref, wa2_ref, ba_ref,
                     q_ref, k_ref, v_ref, g_ref, b_ref, *, kw, gw, q_scale, chunk):
    h = _layer_norm(x_ref[...], lng_ref[...], lnb_ref[...]).astype(BF16)
    z = jnp.dot(h, w_ref[...], preferred_element_type=F32)
    q_ref[...] = z[:, :kw] * q_scale
    k_ref[...] = z[:, kw:2 * kw]
    v_ref[...] = z[:, 2 * kw:2 * kw + gw].astype(BF16)
    g_ref[...] = z[:, 2 * kw + gw:2 * kw + 2 * gw]
    ga = z[:, 2 * kw + 2 * gw:].astype(BF16)
    a = jnp.dot(ga, wa2_ref[...], preferred_element_type=F32) + ba_ref[...]
    log_decay = jax.nn.log_sigmoid(a) * (1.0 / GLA_GATE_TAU)
    b_ref[...] = _chunk_cumsum(log_decay, chunk) if chunk > 1 else log_decay


def _proj_gla(x, lng, lnb, w, wa2, ba, *, tm, kw, gw, q_scale, chunk):
    m, d = x.shape
    assert chunk & (chunk - 1) == 0 and (tm % chunk == 0 or chunk % tm == 0)
    row = lambda n: pl.BlockSpec((tm, n), lambda i: (i, 0))
    return pl.pallas_call(
        functools.partial(_proj_gla_kernel, kw=kw, gw=gw, q_scale=q_scale, chunk=chunk),
        grid=(m // tm,),
        in_specs=[row(d), _full(lng.shape), _full(lnb.shape), _full(w.shape), _full(wa2.shape), _full(ba.shape)],
        out_specs=[row(kw), row(kw), row(gw), row(gw), row(kw)],
        out_shape=[jax.ShapeDtypeStruct((m, kw), F32), jax.ShapeDtypeStruct((m, kw), F32),
                   jax.ShapeDtypeStruct((m, gw), BF16), jax.ShapeDtypeStruct((m, gw), F32),
                   jax.ShapeDtypeStruct((m, kw), F32)],
        compiler_params=_params("parallel"),
        name="proj_gla",
    )(x, lng, lnb, w, wa2, ba)


def _proj_mla_kernel(x_ref, lng_ref, lnb_ref, w_ref, qg_ref, kvg_ref, wuq_ref, wn_ref, cos_ref, sin_ref,
                     q_ref, ckv_ref, kpe_ref, kcat_ref, *, qr, kvr, rope, heads, nope, scale):
    h = _layer_norm(x_ref[...], lng_ref[...], lnb_ref[...]).astype(BF16)
    z = jnp.dot(h, w_ref[...], preferred_element_type=F32)
    cqn = _rms_norm(z[:, :qr], qg_ref[...]).astype(BF16)
    ckvn = _rms_norm(z[:, qr:qr + kvr], kvg_ref[...])
    cos = cos_ref[...]
    sin = sin_ref[...]
    kpe = z[:, qr + kvr:qr + kvr + rope]
    kpe_rot = z[:, qr + kvr + rope:]
    kpe_r = kpe * cos[:, :rope] + kpe_rot * sin[:, :rope]
    ckv_ref[...] = ckvn
    kpe_ref[...] = kpe_r
    kcat_ref[:, :kvr] = ckvn.astype(BF16)
    kcat_ref[:, kvr:] = kpe_r.astype(BF16)
    qf = jnp.dot(cqn, wuq_ref[...], preferred_element_type=F32)
    hn = heads * nope
    hr = heads * rope
    q_pe = (qf[:, hn:hn + hr] * cos + qf[:, hn + hr:] * sin) * scale
    for hd in range(heads):
        q_lat = jnp.dot(qf[:, hd * nope:(hd + 1) * nope].astype(BF16), wn_ref[hd],
                        preferred_element_type=F32) * scale
        q_ref[hd, :, :kvr] = q_lat.astype(BF16)
        q_ref[hd, :, kvr:] = q_pe[:, hd * rope:(hd + 1) * rope].astype(BF16)


def _proj_mla(x, lng, lnb, w, qg, kvg, wuq, wn, cos8, sin8, *, tm, scale):
    m, d = x.shape
    heads, nope, kvr = wn.shape
    qr = qg.shape[-1]
    rope = cos8.shape[-1] // heads
    n_pos_blocks = cos8.shape[0] // tm
    row = lambda n: pl.BlockSpec((tm, n), lambda i: (i, 0))
    tab = pl.BlockSpec((tm, heads * rope), lambda i: (i % n_pos_blocks, 0))
    return pl.pallas_call(
        functools.partial(_proj_mla_kernel, qr=qr, kvr=kvr, rope=rope, heads=heads, nope=nope, scale=scale),
        grid=(m // tm,),
        in_specs=[row(d), _full(lng.shape), _full(lnb.shape), _full(w.shape), _full(qg.shape), _full(kvg.shape),
                  _full(wuq.shape), _full(wn.shape), tab, tab],
        out_specs=[pl.BlockSpec((heads, tm, kvr + rope), lambda i: (0, i, 0)), row(kvr), row(rope), row(kvr + rope)],
        out_shape=[jax.ShapeDtypeStruct((heads, m, kvr + rope), BF16), jax.ShapeDtypeStruct((m, kvr), F32),
                   jax.ShapeDtypeStruct((m, rope), F32), jax.ShapeDtypeStruct((m, kvr + rope), BF16)],
        compiler_params=_params("parallel"),
        name="proj_mla",
    )(x, lng, lnb, w, qg, kvg, wuq, wn, cos8, sin8)


def _mla_prompt_kernel(q_ref, k_ref, kmeta_ref, o_ref, m_sc, l_sc, acc_sc, *, tq, kvr):
    i = pl.program_id(2)
    heads = q_ref.shape[0]
    rows = heads * tq
    q = q_ref[...].reshape(rows, q_ref.shape[2])
    nt = (((1,), (1,)), ((), ()))

    km = kmeta_ref[...]
    s = lax.dot_general(q, km, nt, preferred_element_type=F32)
    m0 = jnp.max(s, axis=-1, keepdims=True)
    p = jnp.exp(s - m0)
    m_sc[...] = jnp.broadcast_to(m0, (rows, LANES))
    lane = lax.broadcasted_iota(jnp.int32, (rows, LANES), 1)
    l_sc[...] = jnp.where(lane == 0, jnp.sum(p, axis=-1, keepdims=True), 0.0)
    acc_sc[...] = jnp.dot(p.astype(BF16), km[:, :kvr], preferred_element_type=F32)

    def update(s, kblk):
        chunks = [s[:, c * LANES:(c + 1) * LANES] for c in range(s.shape[1] // LANES)]
        m_prev = m_sc[...]
        row_max = jnp.max(functools.reduce(jnp.maximum, chunks), axis=-1, keepdims=True)
        m_new = jnp.maximum(m_prev, jnp.broadcast_to(row_max, (rows, LANES)))
        corr = jnp.exp(m_prev - m_new)
        ps = [jnp.exp(c - m_new) for c in chunks]
        l_sc[...] = l_sc[...] * corr + functools.reduce(jnp.add, ps)
        p = jnp.concatenate([x.astype(BF16) for x in ps], axis=1)
        pv = jnp.dot(p, kblk[:, :kvr], preferred_element_type=F32)
        acc_sc[...] = acc_sc[...] * jnp.concatenate([corr] * (kvr // LANES), axis=1) + pv
        m_sc[...] = m_new

    def body(j, carry):
        kblk = k_ref[pl.ds(pl.multiple_of(j * tq, tq), tq), :]
        update(lax.dot_general(q, kblk, nt, preferred_element_type=F32), kblk)
        return carry

    lax.fori_loop(0, i, body, 0)

    kblk = k_ref[pl.ds(pl.multiple_of(i * tq, tq), tq), :]
    s = lax.dot_general(q, kblk, nt, preferred_element_type=F32)
    tok = lax.broadcasted_iota(jnp.int32, (heads, tq, tq), 1).reshape(rows, tq)
    key = lax.broadcasted_iota(jnp.int32, (rows, tq), 1)
    update(jnp.where(key <= tok, s, -jnp.inf), kblk)

    o = acc_sc[...] / jnp.sum(l_sc[...], axis=-1, keepdims=True)
    o_ref[...] = o.reshape(heads, tq, kvr).astype(o_ref.dtype)


def _mla_prompt(q, kcat, kmeta, *, bsz, seq, tq, kvr, heads_per_block):
    heads, m, dq = q.shape
    nq = seq // tq
    rows = heads_per_block * tq
    return pl.pallas_call(
        functools.partial(_mla_prompt_kernel, tq=tq, kvr=kvr),
        grid=(bsz, heads // heads_per_block, nq),
        in_specs=[pl.BlockSpec((heads_per_block, tq, dq), lambda b, h, i: (h, b * nq + i, 0)),
                  pl.BlockSpec((seq, dq), lambda b, h, i: (b, 0)),
                  pl.BlockSpec(kmeta.shape, lambda b, h, i: (0, 0))],
        out_specs=pl.BlockSpec((heads_per_block, tq, kvr), lambda b, h, i: (h, b * nq + i, 0)),
        out_shape=jax.ShapeDtypeStruct((heads, m, kvr), BF16),
        scratch_shapes=[pltpu.VMEM((rows, LANES), F32), pltpu.VMEM((rows, LANES), F32), pltpu.VMEM((rows, kvr), F32)],
        compiler_params=_params("parallel", "parallel", "arbitrary"),
        name="mla_prompt",
    )(q, kcat, kmeta)


def _gla_state_terms(kd, v, decay_last):
    c, dk = kd.shape
    aug = jnp.concatenate([kd, jnp.broadcast_to(decay_last, (8, dk)), jnp.zeros((dk - c - 8, dk), F32)], axis=0)
    aug_t = jnp.transpose(aug)
    return aug_t[:, c:c + 1], jnp.dot(aug_t[:, :c].astype(BF16), v, preferred_element_type=F32)


def _gla_prompt_kernel(q_ref, k_ref, v_ref, b_ref, km_ref, vm_ref, bm_ref, o_ref, sfin_ref, *, chunk):
    seq = q_ref.shape[0]
    n_chunks = seq // chunk

    bm = bm_ref[...]
    bm_last = bm[-1:, :]
    _, s_meta = _gla_state_terms(km_ref[...] * jnp.exp(bm_last - bm), vm_ref[...], jnp.exp(bm_last))

    safe = jnp.max(-b_ref[...]) <= GLA_SAFE_RANGE

    def load(c):
        r0 = pl.multiple_of(c * chunk, chunk)
        rows = pl.ds(r0, chunk)
        return rows, q_ref[rows, :], k_ref[rows, :], v_ref[rows, :], b_ref[rows, :]

    causal = lax.broadcasted_iota(jnp.int32, (chunk, chunk), 1) <= lax.broadcasted_iota(jnp.int32, (chunk, chunk), 0)

    def fast_chunk(c, s):
        rows, q, k, v, b = load(c)
        b_last = b[-1:, :]
        o_inter = jnp.dot((q * jnp.exp(b)).astype(BF16), s.astype(BF16), preferred_element_type=F32)
        kd = k * jnp.exp(b_last - b)
        qd = q * jnp.exp(b - b_last)
        a = lax.dot_general(qd.astype(BF16), kd.astype(BF16), (((1,), (1,)), ((), ())), preferred_element_type=F32)
        a = jnp.where(causal, a, 0.0)
        o_ref[rows, :] = o_inter + jnp.dot(a.astype(BF16), v, preferred_element_type=F32)
        decay_col, kv = _gla_state_terms(kd, v, jnp.exp(b_last))
        return s * decay_col + kv

    def exact_chunk(c, s):
        rows, q, k, v, b = load(c)
        r0 = pl.multiple_of(c * chunk, chunk)
        b_last = b[-1:, :]
        o_ref[rows, :] = jnp.dot((q * jnp.exp(b)).astype(BF16), s.astype(BF16), preferred_element_type=F32)
        vf = v.astype(F32)
        row_id = lax.broadcasted_iota(jnp.int32, (chunk, 1), 0)

        def token(t, carry2):
            sel = row_id == t
            bt = jnp.sum(jnp.where(sel, b, 0.0), axis=0, keepdims=True)
            qt = jnp.sum(jnp.where(sel, q, 0.0), axis=0, keepdims=True)
            w = jnp.exp(jnp.where(row_id <= t, bt - b, -jnp.inf))
            a_col = jnp.sum(w * qt * k, axis=-1, keepdims=True)
            o_t = jnp.sum(a_col * vf, axis=0, keepdims=True)
            o_ref[pl.ds(r0 + t, 1), :] = o_ref[pl.ds(r0 + t, 1), :] + o_t
            return carry2

        lax.fori_loop(0, chunk, token, 0)
        decay_col, kv = _gla_state_terms(k * jnp.exp(b_last - b), v, jnp.exp(b_last))
        return s * decay_col + kv

    @pl.when(safe)
    def _():
        sfin_ref[...] = lax.fori_loop(0, n_chunks, fast_chunk, s_meta, unroll=GLA_UNROLL)

    @pl.when(jnp.logical_not(safe))
    def _():
        sfin_ref[...] = lax.fori_loop(0, n_chunks, exact_chunk, s_meta)


def _gla_prompt(q, k, v, la, km, vm, lam, *, bsz, seq, heads, dk, dv):
    m = q.shape[0]
    nm = km.shape[0]
    col = lambda n: pl.BlockSpec((seq, n), lambda b, h: (b, h))
    mcol = lambda n: pl.BlockSpec((nm, n), lambda b, h: (0, h))
    return pl.pallas_call(
        functools.partial(_gla_prompt_kernel, chunk=GLA_CHUNK),
        grid=(bsz, heads),
        in_specs=[col(dk), col(dk), col(dv), col(dk), mcol(dk), mcol(dv), mcol(dk)],
        out_specs=[pl.BlockSpec((None, seq, dv), lambda b, h: (h, b, 0)),
                   pl.BlockSpec((None, None, dk, dv), lambda b, h: (b, h, 0, 0))],
        out_shape=[jax.ShapeDtypeStruct((heads, m, dv), F32), jax.ShapeDtypeStruct((bsz, heads, dk, dv), F32)],
        compiler_params=_params("parallel", "parallel"),
        name="gla_prompt",
    )(q, k, v, la, km, vm, lam)


def _mla_decode_kernel(pt_ref, q_ref, ckvs_ref, kpes_ref, ckv_hbm, kpe_hbm, o_ref,
                       ckv_buf, kpe_buf, ckv_sem, kpe_sem, *, group, kvr, page):
    nb, n_pages = pt_ref.shape
    steps = n_pages // group
    heads = q_ref.shape[1]

    def copies(b, g, slot):
        out = []
        for p in range(group):
            pid = pt_ref[b, g * group + p]
            out.append(pltpu.make_async_copy(ckv_hbm.at[0, pid], ckv_buf.at[slot, pl.ds(p * page, page), :],
                                             ckv_sem.at[slot]))
            out.append(pltpu.make_async_copy(kpe_hbm.at[0, pid], kpe_buf.at[slot, :, pl.ds(p * page, page)],
                                             kpe_sem.at[slot]))
        return out

    def start(b, g, slot):
        for n, c in enumerate(copies(b, g, slot)):
            c.start(priority=(n // 2) % 2)

    def wait(slot):
        for c in copies(0, 0, slot):
            c.wait()

    b = pl.program_id(0)

    @pl.when(b == 0)
    def _():
        start(0, 0, 0)

    nt = (((1,), (1,)), ((), ()))
    q = q_ref[b]
    ckv_new = ckvs_ref[b]
    kpe_new = kpes_ref[b]
    m = (jnp.sum(q[:, :kvr] * ckv_new, axis=-1, keepdims=True)
         + jnp.sum(q[:, kvr:] * kpe_new, axis=-1, keepdims=True))
    l = jnp.ones_like(m)
    acc = jnp.broadcast_to(ckv_new, (heads, kvr))
    q_lat = q[:, :kvr].astype(BF16)
    q_pe = q[:, kvr:].astype(BF16)
    for g in range(steps):
        slot = g % 2
        wait(slot)
        if g + 1 < steps:
            start(b, g + 1, 1 - slot)
        else:
            @pl.when(b + 1 < nb)
            def _():
                start(b + 1, 0, 1 - slot)
        ck = ckv_buf[slot].astype(BF16)
        kp = kpe_buf[slot].astype(BF16)
        s = (lax.dot_general(q_lat, ck, nt, preferred_element_type=F32)
             + jnp.dot(q_pe, kp, preferred_element_type=F32))
        m_new = jnp.maximum(m, jnp.max(s, axis=-1, keepdims=True))
        corr = jnp.exp(m - m_new)
        p = jnp.exp(s - m_new)
        l = l * corr + jnp.sum(p, axis=-1, keepdims=True)
        acc = acc * corr + jnp.dot(p.astype(BF16), ck, preferred_element_type=F32)
        m = m_new
    o_ref[b] = acc / l


def _decode_group(n_pages):
    group = min(DECODE_GROUP_PAGES, n_pages // 2)
    while n_pages % (2 * group):
        group -= 1
    return group


def _mla_decode(page_table, q, ckv_new, kpe_new, cache_ckv, cache_kpe_t):
    nb, heads, _ = q.shape
    n_pages = page_table.shape[1]
    page, kvr = cache_ckv.shape[2:]
    rope = cache_kpe_t.shape[2]
    group = _decode_group(n_pages)
    vmem = pl.BlockSpec(memory_space=pltpu.VMEM)
    hbm = pl.BlockSpec(memory_space=pl.ANY)
    grid_spec = pltpu.PrefetchScalarGridSpec(
        num_scalar_prefetch=1,
        grid=(nb,),
        in_specs=[vmem, vmem, vmem, hbm, hbm],
        out_specs=vmem,
        scratch_shapes=[pltpu.VMEM((2, group * page, kvr), F32), pltpu.VMEM((2, rope, group * page), F32),
                        pltpu.SemaphoreType.DMA((2,)), pltpu.SemaphoreType.DMA((2,))],
    )
    return pl.pallas_call(
        functools.partial(_mla_decode_kernel, group=group, kvr=kvr, page=page),
        grid_spec=grid_spec,
        out_shape=jax.ShapeDtypeStruct((nb, heads, kvr), F32),
        compiler_params=_params("arbitrary"),
        name="mla_decode",
    )(page_table, q, ckv_new, kpe_new, cache_ckv, cache_kpe_t)


def _gla_decode_kernel(q_ref, k_ref, v_ref, la_ref, s_ref, o_ref, snew_ref, *, heads, dk, dv):
    tb = q_ref.shape[0]
    decay = jnp.exp(la_ref[...])
    q = q_ref[...]
    k = k_ref[...]
    v = v_ref[...].astype(F32)
    qd = (q * decay).astype(BF16)
    head = lambda a, h, n: a[:, h * n:(h + 1) * n]
    rows = jnp.concatenate([head(decay, h, dk) for h in range(heads)] + [head(k, h, dk) for h in range(heads)]
                           + [jnp.zeros((dk - 2 * heads * tb, dk), F32)], axis=0)
    cols = jnp.transpose(rows)
    for bi in range(tb):
        for h in range(heads):
            s_old = s_ref[bi, h]
            qh = head(q, h, dk)[bi:bi + 1]
            kh = head(k, h, dk)[bi:bi + 1]
            vh = head(v, h, dv)[bi:bi + 1]
            qk = jnp.sum(qh * kh, axis=-1, keepdims=True)
            qd8 = jnp.broadcast_to(head(qd, h, dk)[bi:bi + 1], (8, dk))
            o_h = jnp.dot(qd8, s_old.astype(BF16), preferred_element_type=F32)[:1, :] + qk * vh
            o_ref[bi:bi + 1, h * dv:(h + 1) * dv] = o_h
            c_decay = h * tb + bi
            c_key = (heads + h) * tb + bi
            snew_ref[bi, h] = s_old * cols[:, c_decay:c_decay + 1] + cols[:, c_key:c_key + 1] * vh


def _gla_decode(q, k, v, la, state, *, tb):
    nb, heads, dk, dv = state.shape
    row = lambda n: pl.BlockSpec((tb, n), lambda i: (i, 0))
    st = pl.BlockSpec((tb, heads, dk, dv), lambda i: (i, 0, 0, 0))
    return pl.pallas_call(
        functools.partial(_gla_decode_kernel, heads=heads, dk=dk, dv=dv),
        grid=(nb // tb,),
        in_specs=[row(heads * dk), row(heads * dk), row(heads * dv), row(heads * dk), st],
        out_specs=[row(heads * dv), st],
        out_shape=[jax.ShapeDtypeStruct((nb, heads * dv), F32), jax.ShapeDtypeStruct(state.shape, F32)],
        compiler_params=_params("parallel"),
        name="gla_decode",
    )(q, k, v, la, state)


def _mix_out_kernel(x_ref, lng_ref, lnb_ref, og_ref, g_ref, ol_ref, ng_ref, wv_ref, wo_ref, l1g_ref, l1b_ref,
                    h1_ref, mix_sc, *, alpha):
    gla_heads, _, dv = og_ref.shape
    mla_heads, _, vd = wv_ref.shape[0], wv_ref.shape[1], wv_ref.shape[2]
    gw = gla_heads * dv
    gate = g_ref[...]
    for h in range(gla_heads):
        og = _rms_norm(og_ref[h], ng_ref[...])
        gh = gate[:, h * dv:(h + 1) * dv]
        mix_sc[:, h * dv:(h + 1) * dv] = (og * (gh * jax.nn.sigmoid(gh))).astype(BF16)
    for h in range(mla_heads):
        om = jnp.dot(ol_ref[h], wv_ref[h], preferred_element_type=F32)
        mix_sc[:, gw + h * vd:gw + (h + 1) * vd] = om.astype(BF16)
    mix = jnp.dot(mix_sc[...], wo_ref[...], preferred_element_type=F32)
    hres = _layer_norm(x_ref[...], lng_ref[...], lnb_ref[...])
    h1_ref[...] = _layer_norm(alpha * hres + mix, l1g_ref[...], l1b_ref[...])


def _mix_out(x, lng, lnb, og, gate, ol, ng, wv, wo, l1g, l1b, *, tm, alpha):
    m, d = x.shape
    gla_heads, _, dv = og.shape
    mla_heads, _, kvr = ol.shape
    row = lambda n: pl.BlockSpec((tm, n), lambda i: (i, 0))
    return pl.pallas_call(
        functools.partial(_mix_out_kernel, alpha=alpha),
        grid=(m // tm,),
        in_specs=[row(d), _full(lng.shape), _full(lnb.shape),
                  pl.BlockSpec((gla_heads, tm, dv), lambda i: (0, i, 0)), row(gate.shape[1]),
                  pl.BlockSpec((mla_heads, tm, kvr), lambda i: (0, i, 0)),
                  _full(ng.shape), _full(wv.shape), _full(wo.shape), _full(l1g.shape), _full(l1b.shape)],
        out_specs=row(d),
        out_shape=jax.ShapeDtypeStruct((m, d), F32),
        scratch_shapes=[pltpu.VMEM((tm, wo.shape[0]), BF16)],
        compiler_params=_params("parallel"),
        name="mix_out",
    )(x, lng, lnb, og, gate, ol, ng, wv, wo, l1g, l1b)


def _mlp_kernel(h_ref, wu_ref, wd_ref, l2g_ref, l2b_ref, o_ref, hb_sc, *, alpha):
    j = pl.program_id(1)

    @pl.when(j == 0)
    def _():
        hb_sc[...] = h_ref[...].astype(BF16)
        o_ref[...] = jnp.zeros_like(o_ref)

    u = jnp.maximum(jnp.dot(hb_sc[...], wu_ref[...], preferred_element_type=F32), 0.0)
    o_ref[...] += jnp.dot((u * u).astype(BF16), wd_ref[...], preferred_element_type=F32)

    @pl.when(j == pl.num_programs(1) - 1)
    def _():
        o_ref[...] = _layer_norm(alpha * h_ref[...] + o_ref[...], l2g_ref[...], l2b_ref[...])


def _mlp(h1, wu, wd, l2g, l2b, *, tm, tf, alpha):
    m, d = h1.shape
    ff = wu.shape[1]
    return pl.pallas_call(
        functools.partial(_mlp_kernel, alpha=alpha),
        grid=(m // tm, ff // tf),
        in_specs=[pl.BlockSpec((tm, d), lambda i, j: (i, 0)), pl.BlockSpec((d, tf), lambda i, j: (0, j)),
                  pl.BlockSpec((tf, d), lambda i, j: (j, 0)), _full(l2g.shape), _full(l2b.shape)],
        out_specs=pl.BlockSpec((tm, d), lambda i, j: (i, 0)),
        out_shape=jax.ShapeDtypeStruct((m, d), F32),
        scratch_shapes=[pltpu.VMEM((tm, d), BF16)],
        compiler_params=_params("parallel", "arbitrary"),
        name="mlp",
    )(h1, wu, wd, l2g, l2b)


def _rope_tables(pos, rope, heads):
    half = rope // 2
    inv_freq = ROPE_BASE ** (-jnp.arange(half, dtype=F32) / half)
    ang = pos.astype(F32)[:, None] * inv_freq[None, :]
    cos, sin = jnp.cos(ang), jnp.sin(ang)
    return jnp.tile(jnp.concatenate([cos, cos], axis=1), (1, heads)), jnp.tile(jnp.concatenate([sin, sin], axis=1), (1, heads))


def _rotate_half_cols(w):
    half = w.shape[-1] // 2
    return jnp.concatenate([-w[..., half:], w[..., :half]], axis=-1)


def _row_tile(m, cap):
    t = min(m, cap)
    while m % t:
        t //= 2
    return t


def kernel(x_prompt, x_sample, cache_ckv, cache_kpe, state_gla, page_table, meta_tokens, ln_emb_g, ln_emb_b, w_in, w_gla_a2, b_gla_a, gla_norm_g, mla_q_norm_g, w_mla_uq, mla_kv_norm_g, w_mla_ukv, w_out, ln1_g, ln1_b, w_up, w_down, ln2_g, ln2_b):
    bsz, seq, d = x_prompt.shape
    nb, dec_seq, _ = x_sample.shape
    depth = w_in.shape[0]
    assert depth == 1 and dec_seq == 1
    _, gla_heads, dk, dv = state_gla.shape[1:]
    page, kvr = cache_ckv.shape[2:]
    rope = cache_kpe.shape[3]
    n_pages = page_table.shape[1]
    n_meta = meta_tokens.shape[0]
    rank = w_gla_a2.shape[1]
    kw, gw = gla_heads * dk, gla_heads * dv
    qr = mla_q_norm_g.shape[-1]
    mla_heads = w_mla_ukv.shape[2]
    nope = w_mla_uq.shape[-1] // mla_heads - rope
    alpha = (2 * depth) ** 0.25
    mla_scale = (nope + rope) ** -0.5
    assert rank <= LANES and seq % GLA_CHUNK == 0 and n_pages % 2 == 0

    w0 = w_in[0]
    o_ga = 2 * kw + 2 * gw
    o_cq = o_ga + rank
    o_kpe = o_cq + qr + kvr
    w_gla = jnp.concatenate([w0[:, :o_ga], jnp.pad(w0[:, o_ga:o_cq], ((0, 0), (0, LANES - rank)))], axis=1).astype(BF16)
    w_a2 = jnp.pad(w_gla_a2[0], ((0, LANES - rank), (0, 0))).astype(BF16)
    b_a = b_gla_a[0][None, :]
    w_kpe = w0[:, o_kpe:]
    w_mla = jnp.concatenate([w0[:, o_cq:], _rotate_half_cols(w_kpe)], axis=1).astype(BF16)
    uq = w_mla_uq[0].reshape(qr, mla_heads, nope + rope)
    uq_rope = uq[..., nope:]
    w_uq = jnp.concatenate([uq[..., :nope].reshape(qr, mla_heads * nope), uq_rope.reshape(qr, mla_heads * rope),
                            _rotate_half_cols(uq_rope).reshape(qr, mla_heads * rope)], axis=1).astype(BF16)
    ukv = w_mla_ukv[0]
    w_n = jnp.transpose(ukv[..., :nope], (1, 2, 0)).astype(BF16)
    w_v = jnp.transpose(ukv[..., nope:], (1, 0, 2)).astype(BF16)
    w_o = w_out[0].astype(BF16)
    w_u = w_up[0].astype(BF16)
    w_d = w_down[0].astype(BF16)
    lng, lnb = ln_emb_g[None, :], ln_emb_b[None, :]
    qg, kvg, ng = mla_q_norm_g, mla_kv_norm_g, gla_norm_g
    l1g, l1b, l2g, l2b = ln1_g, ln1_b, ln2_g, ln2_b

    cos_m, sin_m = _rope_tables(jnp.arange(n_meta, dtype=jnp.int32), rope, mla_heads)
    cos_p, sin_p = _rope_tables(n_meta + jnp.arange(seq, dtype=jnp.int32), rope, mla_heads)
    cos_s, sin_s = _rope_tables(jnp.full((nb,), n_pages * page, dtype=jnp.int32), rope, mla_heads)

    def project(x2d, cos8, sin8, tm, chunk):
        gla = _proj_gla(x2d, lng, lnb, w_gla, w_a2, b_a, tm=tm, kw=kw, gw=gw, q_scale=dk ** -0.5, chunk=chunk)
        mla = _proj_mla(x2d, lng, lnb, w_mla, qg, kvg, w_uq, w_n, cos8, sin8, tm=tm, scale=mla_scale)
        return gla, mla

    def finish(x2d, o_gla, gate, o_lat, tm, tm_mlp):
        h1 = _mix_out(x2d, lng, lnb, o_gla, gate, o_lat, ng, w_v, w_o, l1g, l1b, tm=tm, alpha=alpha)
        return _mlp(h1, w_u, w_d, l2g, l2b, tm=tm_mlp, tf=512, alpha=alpha)

    xp = x_prompt.reshape(bsz * seq, d)
    (_, km, vm, _, bm), (_, ckv_m, kpe_m, kcat_m) = project(meta_tokens, cos_m, sin_m, n_meta, GLA_CHUNK)
    tm_p = _row_tile(seq, PROJ_ROW_TILE)
    (qg_p, kg_p, vg_p, gate_p, b_p), (q_p, ckv_p, kpe_p, kcat_p) = project(xp, cos_p, sin_p, tm_p, GLA_CHUNK)
    o_gla_p, s_p = _gla_prompt(qg_p, kg_p, vg_p, b_p, km, vm, bm,
                               bsz=bsz, seq=seq, heads=gla_heads, dk=dk, dv=dv)
    o_lat_p = _mla_prompt(q_p, kcat_p, kcat_m, bsz=bsz, seq=seq, tq=_row_tile(seq, MLA_PROMPT_BLOCK), kvr=kvr,
                          heads_per_block=min(mla_heads, MLA_PROMPT_HEADS_PER_BLOCK))
    y_p = finish(xp, o_gla_p, gate_p, o_lat_p, tm_p, _row_tile(bsz * seq, 1024))

    xs = x_sample.reshape(nb, d)
    (qg_s, kg_s, vg_s, gate_s, la_s), (q_s, ckv_s, kpe_s, _) = project(xs, cos_s, sin_s, nb, 1)
    o_gla_s, s_s = _gla_decode(qg_s, kg_s, vg_s, la_s, state_gla[0], tb=_row_tile(nb, 8))
    q_dec = jnp.transpose(q_s, (1, 0, 2)).astype(F32)
    cache_kpe_t = jnp.swapaxes(cache_kpe, 2, 3)
    o_lat_s = _mla_decode(page_table, q_dec, ckv_s[:, None, :], kpe_s[:, None, :], cache_ckv, cache_kpe_t)
    o_gla_s = jnp.transpose(o_gla_s.reshape(nb, gla_heads, dv), (1, 0, 2))
    o_lat_s = jnp.transpose(o_lat_s, (1, 0, 2)).astype(BF16)
    y_s = finish(xs, o_gla_s, gate_s, o_lat_s, nb, nb)

    def with_meta(meta_rows, rows):
        meta_b = jnp.broadcast_to(meta_rows[None], (bsz,) + meta_rows.shape)
        return jnp.concatenate([meta_b, rows.reshape(bsz, seq, rows.shape[-1])], axis=1)[None]

    return (y_p.reshape(bsz, seq, d), y_s.reshape(nb, 1, d),
            with_meta(ckv_m, ckv_p), with_meta(kpe_m, kpe_p), s_p[None],
            ckv_s.reshape(1, nb, 1, kvr), kpe_s.reshape(1, nb, 1, rope), s_s[None])
```

```python
import functools

import jax
import jax.numpy as jnp
from jax import lax
from jax.experimental import pallas as pl
from jax.experimental.pallas import tpu as pltpu

F32 = jnp.float32
BF16 = jnp.bfloat16

LN_EPS = 1e-5
RMS_EPS = 1e-6
ROPE_BASE = 10000.0
GLA_GATE_TAU = 16.0
GLA_CHUNK = 64
GLA_SAFE_RANGE = 60.0
GLA_UNROLL = 4
LANES = 128
VMEM_LIMIT_BYTES = 56 * 1024 * 1024
DECODE_GROUP_PAGES = 32
DECODE_RING_SLOTS = 4
MLA_PROMPT_BLOCK = 512
MLA_PROMPT_HEADS_PER_BLOCK = 4
PROJ_ROW_TILE = 512


def _params(*sem):
    return pltpu.CompilerParams(dimension_semantics=sem, vmem_limit_bytes=VMEM_LIMIT_BYTES)


def _layer_norm(x, g, b):
    mu = jnp.mean(x, axis=-1, keepdims=True)
    xc = x - mu
    var = jnp.mean(xc * xc, axis=-1, keepdims=True)
    return xc * lax.rsqrt(var + LN_EPS) * g + b


def _rms_norm(x, g):
    return x * lax.rsqrt(jnp.mean(x * x, axis=-1, keepdims=True) + RMS_EPS) * g


def _full(shape):
    return pl.BlockSpec(shape, lambda *_: (0,) * len(shape), pipeline_mode=pl.Buffered(1))


def _split3(x):
    hi = x.astype(BF16)
    r1 = x - hi.astype(F32)
    mid = r1.astype(BF16)
    lo = (r1 - mid.astype(F32)).astype(BF16)
    return hi, mid, lo


def _chunk_cumsum(x, chunk):
    rows = x.shape[0]
    sub = min(rows, max(chunk, LANES))
    r = lax.broadcasted_iota(jnp.int32, (sub, sub), 0)
    d = r - lax.broadcasted_iota(jnp.int32, (sub, sub), 1)
    tri = jnp.where((d >= 0) & (d <= (r & (chunk - 1))), 1.0, 0.0).astype(BF16)
    out = []
    for r0 in range(0, rows, sub):
        hi, mid, lo = _split3(x[r0:r0 + sub])
        dot = lambda t: jnp.dot(tri, t, preferred_element_type=F32)
        out.append(dot(hi) + dot(mid) + dot(lo))
    return out[0] if len(out) == 1 else jnp.concatenate(out, axis=0)


def _proj_gla_kernel(x_ref, lng_ref, lnb_ref, w_ref, wa2_ref, ba_ref,
                     q_ref, k_ref, v_ref, g_ref, b_ref, *, kw, gw, q_scale, chunk):
    h = _layer_norm(x_ref[...], lng_ref[...], lnb_ref[...]).astype(BF16)
    z = jnp.dot(h, w_ref[...], preferred_element_type=F32)
    q_ref[...] = z[:, :kw] * q_scale
    k_ref[...] = z[:, kw:2 * kw]
    v_ref[...] = z[:, 2 * kw:2 * kw + gw].astype(BF16)
    g_ref[...] = z[:, 2 * kw + gw:2 * kw + 2 * gw]
    ga = z[:, 2 * kw + 2 * gw:].astype(BF16)
    a = jnp.dot(ga, wa2_ref[...], preferred_element_type=F32) + ba_ref[...]
    log_decay = jax.nn.log_sigmoid(a) * (1.0 / GLA_GATE_TAU)
    b_ref[...] = _chunk_cumsum(log_decay, chunk) if chunk > 1 else log_decay


def _proj_gla(x, lng, lnb, w, wa2, ba, *, tm, kw, gw, q_scale, chunk):
    m, d = x.shape
    assert chunk & (chunk - 1) == 0 and (tm % chunk == 0 or chunk % tm == 0)
    row = lambda n: pl.BlockSpec((tm, n), lambda i: (i, 0))
    return pl.pallas_call(
        functools.partial(_proj_gla_kernel, kw=kw, gw=gw, q_scale=q_scale, chunk=chunk),
        grid=(m // tm,),
        in_specs=[row(d), _full(lng.shape), _full(lnb.shape), _full(w.shape), _full(wa2.shape), _full(ba.shape)],
        out_specs=[row(kw), row(kw), row(gw), row(gw), row(kw)],
        out_shape=[jax.ShapeDtypeStruct((m, kw), F32), jax.ShapeDtypeStruct((m, kw), F32),
                   jax.ShapeDtypeStruct((m, gw), BF16), jax.ShapeDtypeStruct((m, gw), F32),
                   jax.ShapeDtypeStruct((m, kw), F32)],
        compiler_params=_params("parallel"),
        name="proj_gla",
    )(x, lng, lnb, w, wa2, ba)


def _proj_mla_kernel(x_ref, lng_ref, lnb_ref, w_ref, qg_ref, kvg_ref, wuq_ref, wn_ref, cos_ref, sin_ref,
                     q_ref, ckv_ref, kpe_ref, kcat_ref, *, qr, kvr, rope, heads, nope, scale):
    h = _layer_norm(x_ref[...], lng_ref[...], lnb_ref[...]).astype(BF16)
    z = jnp.dot(h, w_ref[...], preferred_element_type=F32)
    cqn = _rms_norm(z[:, :qr], qg_ref[...]).astype(BF16)
    ckvn = _rms_norm(z[:, qr:qr + kvr], kvg_ref[...])
    cos = cos_ref[...]
    sin = sin_ref[...]
    kpe = z[:, qr + kvr:qr + kvr + rope]
    kpe_rot = z[:, qr + kvr + rope:]
    kpe_r = kpe * cos[:, :rope] + kpe_rot * sin[:, :rope]
    ckv_ref[...] = ckvn
    kpe_ref[...] = kpe_r
    kcat_ref[:, :kvr] = ckvn.astype(BF16)
    kcat_ref[:, kvr:] = kpe_r.astype(BF16)
    qf = jnp.dot(cqn, wuq_ref[...], preferred_element_type=F32)
    hn = heads * nope
    hr = heads * rope
    q_pe = (qf[:, hn:hn + hr] * cos + qf[:, hn + hr:] * sin) * scale
    for hd in range(heads):
        q_lat = jnp.dot(qf[:, hd * nope:(hd + 1) * nope].astype(BF16), wn_ref[hd],
                        preferred_element_type=F32) * scale
        q_ref[hd, :, :kvr] = q_lat.astype(BF16)
        q_ref[hd, :, kvr:] = q_pe[:, hd * rope:(hd + 1) * rope].astype(BF16)


def _proj_mla(x, lng, lnb, w, qg, kvg, wuq, wn, cos8, sin8, *, tm, scale):
    m, d = x.shape
    heads, nope, kvr = wn.shape
    qr = qg.shape[-1]
    rope = cos8.shape[-1] // heads
    n_pos_blocks = cos8.shape[0] // tm
    row = lambda n: pl.BlockSpec((tm, n), lambda i: (i, 0))
    tab = pl.BlockSpec((tm, heads * rope), lambda i: (i % n_pos_blocks, 0))
    return pl.pallas_call(
        functools.partial(_proj_mla_kernel, qr=qr, kvr=kvr, rope=rope, heads=heads, nope=nope, scale=scale),
        grid=(m // tm,),
        in_specs=[row(d), _full(lng.shape), _full(lnb.shape), _full(w.shape), _full(qg.shape), _full(kvg.shape),
                  _full(wuq.shape), _full(wn.shape), tab, tab],
        out_specs=[pl.BlockSpec((heads, tm, kvr + rope), lambda i: (0, i, 0)), row(kvr), row(rope), row(kvr + rope)],
        out_shape=[jax.ShapeDtypeStruct((heads, m, kvr + rope), BF16), jax.ShapeDtypeStruct((m, kvr), F32),
                   jax.ShapeDtypeStruct((m, rope), F32), jax.ShapeDtypeStruct((m, kvr + rope), BF16)],
        compiler_params=_params("parallel"),
        name="proj_mla",
    )(x, lng, lnb, w, qg, kvg, wuq, wn, cos8, sin8)


def _mla_prompt_kernel(q_ref, k_ref, kmeta_ref, o_ref, m_sc, l_sc, acc_sc, *, tq, kvr):
    i = pl.program_id(2)
    heads = q_ref.shape[0]
    rows = heads * tq
    q = q_ref[...].reshape(rows, q_ref.shape[2])
    nt = (((1,), (1,)), ((), ()))

    km = kmeta_ref[...]
    s = lax.dot_general(q, km, nt, preferred_element_type=F32)
    m0 = jnp.max(s, axis=-1, keepdims=True)
    p = jnp.exp(s - m0)
    m_sc[...] = jnp.broadcast_to(m0, (rows, LANES))
    lane = lax.broadcasted_iota(jnp.int32, (rows, LANES), 1)
    l_sc[...] = jnp.where(lane == 0, jnp.sum(p, axis=-1, keepdims=True), 0.0)
    acc_sc[...] = jnp.dot(p.astype(BF16), km[:, :kvr], preferred_element_type=F32)

    def update(s, kblk):
        chunks = [s[:, c * LANES:(c + 1) * LANES] for c in range(s.shape[1] // LANES)]
        m_prev = m_sc[...]
        row_max = jnp.max(functools.reduce(jnp.maximum, chunks), axis=-1, keepdims=True)
        m_new = jnp.maximum(m_prev, jnp.broadcast_to(row_max, (rows, LANES)))
        corr = jnp.exp(m_prev - m_new)
        ps = [jnp.exp(c - m_new) for c in chunks]
        l_sc[...] = l_sc[...] * corr + functools.reduce(jnp.add, ps)
        p = jnp.concatenate([x.astype(BF16) for x in ps], axis=1)
        pv = jnp.dot(p, kblk[:, :kvr], preferred_element_type=F32)
        acc_sc[...] = acc_sc[...] * jnp.concatenate([corr] * (kvr // LANES), axis=1) + pv
        m_sc[...] = m_new

    def body(j, carry):
        kblk = k_ref[pl.ds(pl.multiple_of(j * tq, tq), tq), :]
        update(lax.dot_general(q, kblk, nt, preferred_element_type=F32), kblk)
        return carry

    lax.fori_loop(0, i, body, 0)

    kblk = k_ref[pl.ds(pl.multiple_of(i * tq, tq), tq), :]
    s = lax.dot_general(q, kblk, nt, preferred_element_type=F32)
    tok = lax.broadcasted_iota(jnp.int32, (heads, tq, tq), 1).reshape(rows, tq)
    key = lax.broadcasted_iota(jnp.int32, (rows, tq), 1)
    update(jnp.where(key <= tok, s, -jnp.inf), kblk)

    o = acc_sc[...] / jnp.sum(l_sc[...], axis=-1, keepdims=True)
    o_ref[...] = o.reshape(heads, tq, kvr).astype(o_ref.dtype)


def _mla_prompt(q, kcat, kmeta, *, bsz, seq, tq, kvr, heads_per_block):
    heads, m, dq = q.shape
    nq = seq // tq
    rows = heads_per_block * tq
    return pl.pallas_call(
        functools.partial(_mla_prompt_kernel, tq=tq, kvr=kvr),
        grid=(bsz, heads // heads_per_block, nq),
        in_specs=[pl.BlockSpec((heads_per_block, tq, dq), lambda b, h, i: (h, b * nq + i, 0)),
                  pl.BlockSpec((seq, dq), lambda b, h, i: (b, 0)),
                  pl.BlockSpec(kmeta.shape, lambda b, h, i: (0, 0))],
        out_specs=pl.BlockSpec((heads_per_block, tq, kvr), lambda b, h, i: (h, b * nq + i, 0)),
        out_shape=jax.ShapeDtypeStruct((heads, m, kvr), BF16),
        scratch_shapes=[pltpu.VMEM((rows, LANES), F32), pltpu.VMEM((rows, LANES), F32), pltpu.VMEM((rows, kvr), F32)],
        compiler_params=_params("parallel", "parallel", "arbitrary"),
        name="mla_prompt",
    )(q, kcat, kmeta)


def _gla_state_terms(kd, v, decay_last):
    c, dk = kd.shape
    aug = jnp.concatenate([kd, jnp.broadcast_to(decay_last, (8, dk)), jnp.zeros((dk - c - 8, dk), F32)], axis=0)
    aug_t = jnp.transpose(aug)
    return aug_t[:, c:c + 1], jnp.dot(aug_t[:, :c].astype(BF16), v, preferred_element_type=F32)


def _gla_prompt_kernel(q_ref, k_ref, v_ref, b_ref, km_ref, vm_ref, bm_ref, o_ref, sfin_ref, *, chunk):
    seq = q_ref.shape[0]
    n_chunks = seq // chunk

    bm = bm_ref[...]
    bm_last = bm[-1:, :]
    _, s_meta = _gla_state_terms(km_ref[...] * jnp.exp(bm_last - bm), vm_ref[...], jnp.exp(bm_last))

    safe = jnp.max(-b_ref[...]) <= GLA_SAFE_RANGE

    def load(c):
        r0 = pl.multiple_of(c * chunk, chunk)
        rows = pl.ds(r0, chunk)
        return rows, q_ref[rows, :], k_ref[rows, :], v_ref[rows, :], b_ref[rows, :]

    causal = lax.broadcasted_iota(jnp.int32, (chunk, chunk), 1) <= lax.broadcasted_iota(jnp.int32, (chunk, chunk), 0)

    def fast_chunk(c, s):
        rows, q, k, v, b = load(c)
        b_last = b[-1:, :]
        o_inter = jnp.dot((q * jnp.exp(b)).astype(BF16), s.astype(BF16), preferred_element_type=F32)
        kd = k * jnp.exp(b_last - b)
        qd = q * jnp.exp(b - b_last)
        a = lax.dot_general(qd.astype(BF16), kd.astype(BF16), (((1,), (1,)), ((), ())), preferred_element_type=F32)
        a = jnp.where(causal, a, 0.0)
        o_ref[rows, :] = o_inter + jnp.dot(a.astype(BF16), v, preferred_element_type=F32)
        decay_col, kv = _gla_state_terms(kd, v, jnp.exp(b_last))
        return s * decay_col + kv

    def exact_chunk(c, s):
        rows, q, k, v, b = load(c)
        r0 = pl.multiple_of(c * chunk, chunk)
        b_last = b[-1:, :]
        o_ref[rows, :] = jnp.dot((q * jnp.exp(b)).astype(BF16), s.astype(BF16), preferred_element_type=F32)
        vf = v.astype(F32)
        row_id = lax.broadcasted_iota(jnp.int32, (chunk, 1), 0)

        def token(t, carry2):
            sel = row_id == t
            bt = jnp.sum(jnp.where(sel, b, 0.0), axis=0, keepdims=True)
            qt = jnp.sum(jnp.where(sel, q, 0.0), axis=0, keepdims=True)
            w = jnp.exp(jnp.where(row_id <= t, bt - b, -jnp.inf))
            a_col = jnp.sum(w * qt * k, axis=-1, keepdims=True)
            o_t = jnp.sum(a_col * vf, axis=0, keepdims=True)
            o_ref[pl.ds(r0 + t, 1), :] = o_ref[pl.ds(r0 + t, 1), :] + o_t
            return carry2

        lax.fori_loop(0, chunk, token, 0)
        decay_col, kv = _gla_state_terms(k * jnp.exp(b_last - b), v, jnp.exp(b_last))
        return s * decay_col + kv

    @pl.when(safe)
    def _():
        sfin_ref[...] = lax.fori_loop(0, n_chunks, fast_chunk, s_meta, unroll=GLA_UNROLL)

    @pl.when(jnp.logical_not(safe))
    def _():
        sfin_ref[...] = lax.fori_loop(0, n_chunks, exact_chunk, s_meta)


def _gla_prompt(q, k, v, la, km, vm, lam, *, bsz, seq, heads, dk, dv):
    m = q.shape[0]
    nm = km.shape[0]
    col = lambda n: pl.BlockSpec((seq, n), lambda b, h: (b, h))
    mcol = lambda n: pl.BlockSpec((nm, n), lambda b, h: (0, h))
    return pl.pallas_call(
        functools.partial(_gla_prompt_kernel, chunk=GLA_CHUNK),
        grid=(bsz, heads),
        in_specs=[col(dk), col(dk), col(dv), col(dk), mcol(dk), mcol(dv), mcol(dk)],
        out_specs=[pl.BlockSpec((None, seq, dv), lambda b, h: (h, b, 0)),
                   pl.BlockSpec((None, None, dk, dv), lambda b, h: (b, h, 0, 0))],
        out_shape=[jax.ShapeDtypeStruct((heads, m, dv), F32), jax.ShapeDtypeStruct((bsz, heads, dk, dv), F32)],
        compiler_params=_params("parallel", "parallel"),
        name="gla_prompt",
    )(q, k, v, la, km, vm, lam)


def _mla_decode_kernel(pt_ref, q_ref, ckvs_ref, kpes_ref, ckv_hbm, kpe_hbm, o_ref,
                       ckv_buf, kpe_buf, ckv_sem, kpe_sem, *, group, slots, kvr, page):
    nb, n_pages = pt_ref.shape
    steps = n_pages // group
    ahead = slots - 1
    heads = q_ref.shape[1]

    def copies(b, g, slot):
        out = []
        for p in range(group):
            pid = pt_ref[b, g * group + p]
            out.append(pltpu.make_async_copy(ckv_hbm.at[0, pid], ckv_buf.at[slot, pl.ds(p * page, page), :],
                                             ckv_sem.at[slot]))
            out.append(pltpu.make_async_copy(kpe_hbm.at[0, pid], kpe_buf.at[slot, p], kpe_sem.at[slot]))
        return out

    def start(b, g, slot):
        for c in copies(b, g, slot):
            c.start()

    def wait(slot):
        for c in copies(0, 0, slot):
            c.wait()

    def start_ahead(b, g):
        t = g + ahead
        if t < steps:
            start(b, t, t % slots)
        else:
            @pl.when(b + 1 < nb)
            def _():
                start(b + 1, t - steps, (t - steps) % slots)

    b = pl.program_id(0)

    @pl.when(b == 0)
    def _():
        for t in range(ahead):
            start(0, t, t % slots)

    nt = (((1,), (1,)), ((), ()))
    q = q_ref[b]
    ckv_new = ckvs_ref[b]
    kpe_new = kpes_ref[b]
    m = (jnp.sum(q[:, :kvr] * ckv_new, axis=-1, keepdims=True)
         + jnp.sum(q[:, kvr:] * kpe_new, axis=-1, keepdims=True))
    l = jnp.ones_like(m)
    acc = jnp.broadcast_to(ckv_new, (heads, kvr))
    q_lat = q[:, :kvr].astype(BF16)
    q_pe = q[:, kvr:].astype(BF16)
    for g in range(steps):
        slot = g % slots
        wait(slot)
        start_ahead(b, g)
        ck = ckv_buf[slot].astype(BF16)
        kp = jnp.concatenate([kpe_buf[slot, p] for p in range(group)], axis=1).astype(BF16)
        s = (lax.dot_general(q_lat, ck, nt, preferred_element_type=F32)
             + jnp.dot(q_pe, kp, preferred_element_type=F32))
        m_new = jnp.maximum(m, jnp.max(s, axis=-1, keepdims=True))
        corr = jnp.exp(m - m_new)
        p = jnp.exp(s - m_new)
        l = l * corr + jnp.sum(p, axis=-1, keepdims=True)
        acc = acc * corr + jnp.dot(p.astype(BF16), ck, preferred_element_type=F32)
        m = m_new
    o_ref[b] = acc / l


def _decode_group(n_pages, slots):
    group = min(DECODE_GROUP_PAGES, n_pages // slots)
    while n_pages % (slots * group):
        group -= 1
    return group


def _mla_decode(page_table, q, ckv_new, kpe_new, cache_ckv, cache_kpe_t):
    slots = DECODE_RING_SLOTS
    nb, heads, _ = q.shape
    n_pages = page_table.shape[1]
    page, kvr = cache_ckv.shape[2:]
    rope = cache_kpe_t.shape[2]
    group = _decode_group(n_pages, slots)
    vmem = pl.BlockSpec(memory_space=pltpu.VMEM)
    hbm = pl.BlockSpec(memory_space=pl.ANY)
    grid_spec = pltpu.PrefetchScalarGridSpec(
        num_scalar_prefetch=1,
        grid=(nb,),
        in_specs=[vmem, vmem, vmem, hbm, hbm],
        out_specs=vmem,
        scratch_shapes=[pltpu.VMEM((slots, group * page, kvr), F32), pltpu.VMEM((slots, group, rope, page), F32),
                        pltpu.SemaphoreType.DMA((slots,)), pltpu.SemaphoreType.DMA((slots,))],
    )
    return pl.pallas_call(
        functools.partial(_mla_decode_kernel, group=group, slots=slots, kvr=kvr, page=page),
        grid_spec=grid_spec,
        out_shape=jax.ShapeDtypeStruct((nb, heads, kvr), F32),
        compiler_params=_params("arbitrary"),
        name="mla_decode",
    )(page_table, q, ckv_new, kpe_new, cache_ckv, cache_kpe_t)


def _gla_decode_kernel(q_ref, k_ref, v_ref, la_ref, s_ref, o_ref, snew_ref, *, heads, dk, dv):
    tb = q_ref.shape[0]
    decay = jnp.exp(la_ref[...])
    q = q_ref[...]
    k = k_ref[...]
    v = v_ref[...].astype(F32)
    qd = (q * decay).astype(BF16)
    head = lambda a, h, n: a[:, h * n:(h + 1) * n]
    rows = jnp.concatenate([head(decay, h, dk) for h in range(heads)] + [head(k, h, dk) for h in range(heads)]
                           + [jnp.zeros((dk - 2 * heads * tb, dk), F32)], axis=0)
    cols = jnp.transpose(rows)
    for bi in range(tb):
        for h in range(heads):
            s_old = s_ref[bi, h]
            qh = head(q, h, dk)[bi:bi + 1]
            kh = head(k, h, dk)[bi:bi + 1]
            vh = head(v, h, dv)[bi:bi + 1]
            qk = jnp.sum(qh * kh, axis=-1, keepdims=True)
            qd8 = jnp.broadcast_to(head(qd, h, dk)[bi:bi + 1], (8, dk))
            o_h = jnp.dot(qd8, s_old.astype(BF16), preferred_element_type=F32)[:1, :] + qk * vh
            o_ref[bi:bi + 1, h * dv:(h + 1) * dv] = o_h
            c_decay = h * tb + bi
            c_key = (heads + h) * tb + bi
            snew_ref[bi, h] = s_old * cols[:, c_decay:c_decay + 1] + cols[:, c_key:c_key + 1] * vh


def _gla_decode(q, k, v, la, state, *, tb):
    nb, heads, dk, dv = state.shape
    row = lambda n: pl.BlockSpec((tb, n), lambda i: (i, 0))
    st = pl.BlockSpec((tb, heads, dk, dv), lambda i: (i, 0, 0, 0))
    return pl.pallas_call(
        functools.partial(_gla_decode_kernel, heads=heads, dk=dk, dv=dv),
        grid=(nb // tb,),
        in_specs=[row(heads * dk), row(heads * dk), row(heads * dv), row(heads * dk), st],
        out_specs=[row(heads * dv), st],
        out_shape=[jax.ShapeDtypeStruct((nb, heads * dv), F32), jax.ShapeDtypeStruct(state.shape, F32)],
        compiler_params=_params("parallel"),
        name="gla_decode",
    )(q, k, v, la, state)


def _mix_out_kernel(x_ref, lng_ref, lnb_ref, og_ref, g_ref, ol_ref, ng_ref, wv_ref, wo_ref, l1g_ref, l1b_ref,
                    h1_ref, mix_sc, *, alpha):
    gla_heads, _, dv = og_ref.shape
    mla_heads, _, vd = wv_ref.shape[0], wv_ref.shape[1], wv_ref.shape[2]
    gw = gla_heads * dv
    gate = g_ref[...]
    for h in range(gla_heads):
        og = _rms_norm(og_ref[h], ng_ref[...])
        gh = gate[:, h * dv:(h + 1) * dv]
        mix_sc[:, h * dv:(h + 1) * dv] = (og * (gh * jax.nn.sigmoid(gh))).astype(BF16)
    for h in range(mla_heads):
        om = jnp.dot(ol_ref[h], wv_ref[h], preferred_element_type=F32)
        mix_sc[:, gw + h * vd:gw + (h + 1) * vd] = om.astype(BF16)
    mix = jnp.dot(mix_sc[...], wo_ref[...], preferred_element_type=F32)
    hres = _layer_norm(x_ref[...], lng_ref[...], lnb_ref[...])
    h1_ref[...] = _layer_norm(alpha * hres + mix, l1g_ref[...], l1b_ref[...])


def _mix_out(x, lng, lnb, og, gate, ol, ng, wv, wo, l1g, l1b, *, tm, alpha):
    m, d = x.shape
    gla_heads, _, dv = og.shape
    mla_heads, _, kvr = ol.shape
    row = lambda n: pl.BlockSpec((tm, n), lambda i: (i, 0))
    return pl.pallas_call(
        functools.partial(_mix_out_kernel, alpha=alpha),
        grid=(m // tm,),
        in_specs=[row(d), _full(lng.shape), _full(lnb.shape),
                  pl.BlockSpec((gla_heads, tm, dv), lambda i: (0, i, 0)), row(gate.shape[1]),
                  pl.BlockSpec((mla_heads, tm, kvr), lambda i: (0, i, 0)),
                  _full(ng.shape), _full(wv.shape), _full(wo.shape), _full(l1g.shape), _full(l1b.shape)],
        out_specs=row(d),
        out_shape=jax.ShapeDtypeStruct((m, d), F32),
        scratch_shapes=[pltpu.VMEM((tm, wo.shape[0]), BF16)],
        compiler_params=_params("parallel"),
        name="mix_out",
    )(x, lng, lnb, og, gate, ol, ng, wv, wo, l1g, l1b)


def _mlp_kernel(h_ref, wu_ref, wd_ref, l2g_ref, l2b_ref, o_ref, hb_sc, *, alpha):
    j = pl.program_id(1)

    @pl.when(j == 0)
    def _():
        hb_sc[...] = h_ref[...].astype(BF16)
        o_ref[...] = jnp.zeros_like(o_ref)

    u = jnp.maximum(jnp.dot(hb_sc[...], wu_ref[...], preferred_element_type=F32), 0.0)
    o_ref[...] += jnp.dot((u * u).astype(BF16), wd_ref[...], preferred_element_type=F32)

    @pl.when(j == pl.num_programs(1) - 1)
    def _():
        o_ref[...] = _layer_norm(alpha * h_ref[...] + o_ref[...], l2g_ref[...], l2b_ref[...])


def _mlp(h1, wu, wd, l2g, l2b, *, tm, tf, alpha):
    m, d = h1.shape
    ff = wu.shape[1]
    return pl.pallas_call(
        functools.partial(_mlp_kernel, alpha=alpha),
        grid=(m // tm, ff // tf),
        in_specs=[pl.BlockSpec((tm, d), lambda i, j: (i, 0)), pl.BlockSpec((d, tf), lambda i, j: (0, j)),
                  pl.BlockSpec((tf, d), lambda i, j: (j, 0)), _full(l2g.shape), _full(l2b.shape)],
        out_specs=pl.BlockSpec((tm, d), lambda i, j: (i, 0)),
        out_shape=jax.ShapeDtypeStruct((m, d), F32),
        scratch_shapes=[pltpu.VMEM((tm, d), BF16)],
        compiler_params=_params("parallel", "arbitrary"),
        name="mlp",
    )(h1, wu, wd, l2g, l2b)


def _rope_tables(pos, rope, heads):
    half = rope // 2
    inv_freq = ROPE_BASE ** (-jnp.arange(half, dtype=F32) / half)
    ang = pos.astype(F32)[:, None] * inv_freq[None, :]
    cos, sin = jnp.cos(ang), jnp.sin(ang)
    return jnp.tile(jnp.concatenate([cos, cos], axis=1), (1, heads)), jnp.tile(jnp.concatenate([sin, sin], axis=1), (1, heads))


def _rotate_half_cols(w):
    half = w.shape[-1] // 2
    return jnp.concatenate([-w[..., half:], w[..., :half]], axis=-1)


def _row_tile(m, cap):
    t = min(m, cap)
    while m % t:
        t //= 2
    return t


def kernel(x_prompt, x_sample, cache_ckv, cache_kpe, state_gla, page_table, meta_tokens, ln_emb_g, ln_emb_b, w_in, w_gla_a2, b_gla_a, gla_norm_g, mla_q_norm_g, w_mla_uq, mla_kv_norm_g, w_mla_ukv, w_out, ln1_g, ln1_b, w_up, w_down, ln2_g, ln2_b):
    bsz, seq, d = x_prompt.shape
    nb, dec_seq, _ = x_sample.shape
    depth = w_in.shape[0]
    assert depth == 1 and dec_seq == 1
    _, gla_heads, dk, dv = state_gla.shape[1:]
    page, kvr = cache_ckv.shape[2:]
    rope = cache_kpe.shape[3]
    n_pages = page_table.shape[1]
    n_meta = meta_tokens.shape[0]
    rank = w_gla_a2.shape[1]
    kw, gw = gla_heads * dk, gla_heads * dv
    qr = mla_q_norm_g.shape[-1]
    mla_heads = w_mla_ukv.shape[2]
    nope = w_mla_uq.shape[-1] // mla_heads - rope
    alpha = (2 * depth) ** 0.25
    mla_scale = (nope + rope) ** -0.5
    assert rank <= LANES and seq % GLA_CHUNK == 0 and n_pages % 2 == 0

    w0 = w_in[0]
    o_ga = 2 * kw + 2 * gw
    o_cq = o_ga + rank
    o_kpe = o_cq + qr + kvr
    w_gla = jnp.concatenate([w0[:, :o_ga], jnp.pad(w0[:, o_ga:o_cq], ((0, 0), (0, LANES - rank)))], axis=1).astype(BF16)
    w_a2 = jnp.pad(w_gla_a2[0], ((0, LANES - rank), (0, 0))).astype(BF16)
    b_a = b_gla_a[0][None, :]
    w_kpe = w0[:, o_kpe:]
    w_mla = jnp.concatenate([w0[:, o_cq:], _rotate_half_cols(w_kpe)], axis=1).astype(BF16)
    uq = w_mla_uq[0].reshape(qr, mla_heads, nope + rope)
    uq_rope = uq[..., nope:]
    w_uq = jnp.concatenate([uq[..., :nope].reshape(qr, mla_heads * nope), uq_rope.reshape(qr, mla_heads * rope),
                            _rotate_half_cols(uq_rope).reshape(qr, mla_heads * rope)], axis=1).astype(BF16)
    ukv = w_mla_ukv[0]
    w_n = jnp.transpose(ukv[..., :nope], (1, 2, 0)).astype(BF16)
    w_v = jnp.transpose(ukv[..., nope:], (1, 0, 2)).astype(BF16)
    w_o = w_out[0].astype(BF16)
    w_u = w_up[0].astype(BF16)
    w_d = w_down[0].astype(BF16)
    lng, lnb = ln_emb_g[None, :], ln_emb_b[None, :]
    qg, kvg, ng = mla_q_norm_g, mla_kv_norm_g, gla_norm_g
    l1g, l1b, l2g, l2b = ln1_g, ln1_b, ln2_g, ln2_b

    cos_m, sin_m = _rope_tables(jnp.arange(n_meta, dtype=jnp.int32), rope, mla_heads)
    cos_p, sin_p = _rope_tables(n_meta + jnp.arange(seq, dtype=jnp.int32), rope, mla_heads)
    cos_s, sin_s = _rope_tables(jnp.full((nb,), n_pages * page, dtype=jnp.int32), rope, mla_heads)

    def project(x2d, cos8, sin8, tm, chunk):
        gla = _proj_gla(x2d, lng, lnb, w_gla, w_a2, b_a, tm=tm, kw=kw, gw=gw, q_scale=dk ** -0.5, chunk=chunk)
        mla = _proj_mla(x2d, lng, lnb, w_mla, qg, kvg, w_uq, w_n, cos8, sin8, tm=tm, scale=mla_scale)
        return gla, mla

    def finish(x2d, o_gla, gate, o_lat, tm, tm_mlp):
        h1 = _mix_out(x2d, lng, lnb, o_gla, gate, o_lat, ng, w_v, w_o, l1g, l1b, tm=tm, alpha=alpha)
        return _mlp(h1, w_u, w_d, l2g, l2b, tm=tm_mlp, tf=512, alpha=alpha)

    xp = x_prompt.reshape(bsz * seq, d)
    (_, km, vm, _, bm), (_, ckv_m, kpe_m, kcat_m) = project(meta_tokens, cos_m, sin_m, n_meta, GLA_CHUNK)
    tm_p = _row_tile(seq, PROJ_ROW_TILE)
    (qg_p, kg_p, vg_p, gate_p, b_p), (q_p, ckv_p, kpe_p, kcat_p) = project(xp, cos_p, sin_p, tm_p, GLA_CHUNK)
    o_gla_p, s_p = _gla_prompt(qg_p, kg_p, vg_p, b_p, km, vm, bm,
                               bsz=bsz, seq=seq, heads=gla_heads, dk=dk, dv=dv)
    o_lat_p = _mla_prompt(q_p, kcat_p, kcat_m, bsz=bsz, seq=seq, tq=_row_tile(seq, MLA_PROMPT_BLOCK), kvr=kvr,
                          heads_per_block=min(mla_heads, MLA_PROMPT_HEADS_PER_BLOCK))
    y_p = finish(xp, o_gla_p, gate_p, o_lat_p, tm_p, _row_tile(bsz * seq, 1024))

    xs = x_sample.reshape(nb, d)
    (qg_s, kg_s, vg_s, gate_s, la_s), (q_s, ckv_s, kpe_s, _) = project(xs, cos_s, sin_s, nb, 1)
    o_gla_s, s_s = _gla_decode(qg_s, kg_s, vg_s, la_s, state_gla[0], tb=_row_tile(nb, 8))
    q_dec = jnp.transpose(q_s, (1, 0, 2)).astype(F32)
    cache_kpe_t = jnp.swapaxes(cache_kpe, 2, 3)
    o_lat_s = _mla_decode(page_table, q_dec, ckv_s[:, None, :], kpe_s[:, None, :], cache_ckv, cache_kpe_t)
    o_gla_s = jnp.transpose(o_gla_s.reshape(nb, gla_heads, dv), (1, 0, 2))
    o_lat_s = jnp.transpose(o_lat_s, (1, 0, 2)).astype(BF16)
    y_s = finish(xs, o_gla_s, gate_s, o_lat_s, nb, nb)

    def with_meta(meta_rows, rows):
        meta_b = jnp.broadcast_to(meta_rows[None], (bsz,) + meta_rows.shape)
        return jnp.concatenate([meta_b, rows.reshape(bsz, seq, rows.shape[-1])], axis=1)[None]

    return (y_p.reshape(bsz, seq, d), y_s.reshape(nb, 1, d),
            with_meta(ckv_m, ckv_p), with_meta(kpe_m, kpe_p), s_p[None],
            ckv_s.reshape(1, nb, 1, kvr), kpe_s.reshape(1, nb, 1, rope), s_s[None])
```

```python
import functools

import jax
import jax.numpy as jnp
from jax import lax
from jax.experimental import pallas as pl
from jax.experimental.pallas import tpu as pltpu

F32 = jnp.float32
BF16 = jnp.bfloat16

LN_EPS = 1e-5
RMS_EPS = 1e-6
ROPE_BASE = 10000.0
GLA_GATE_TAU = 16.0
GLA_CHUNK = 64
GLA_SAFE_RANGE = 60.0
GLA_UNROLL = 8
LANES = 128
VMEM_LIMIT_BYTES = 56 * 1024 * 1024
DECODE_GROUP_PAGES = 32
DECODE_RING_SLOTS = 4
MLA_PROMPT_BLOCK = 512
MLA_PROMPT_HEADS_PER_BLOCK = 4
PROJ_ROW_TILE = 512
MLP_ROW_TILE = 1024
MLP_FF_TILE = 512


_NT = (((1,), (1,)), ((), ()))


def _params(*sem):
    return pltpu.CompilerParams(dimension_semantics=sem, vmem_limit_bytes=VMEM_LIMIT_BYTES)


def _layer_norm(x, g, b):
    mu = jnp.mean(x, axis=-1, keepdims=True)
    xc = x - mu
    var = jnp.mean(xc * xc, axis=-1, keepdims=True)
    return xc * lax.rsqrt(var + LN_EPS) * g + b


def _rms_norm(x, g):
    return x * lax.rsqrt(jnp.mean(x * x, axis=-1, keepdims=True) + RMS_EPS) * g


def _full(shape):
    return pl.BlockSpec(shape, lambda *_: (0,) * len(shape), pipeline_mode=pl.Buffered(1))


def _split3(x):
    hi = x.astype(BF16)
    r1 = x - hi.astype(F32)
    mid = r1.astype(BF16)
    lo = (r1 - mid.astype(F32)).astype(BF16)
    return hi, mid, lo


def _chunk_cumsum(x, chunk):
    rows = x.shape[0]
    sub = min(rows, max(chunk, LANES))
    r = lax.broadcasted_iota(jnp.int32, (sub, sub), 0)
    d = r - lax.broadcasted_iota(jnp.int32, (sub, sub), 1)
    tri = jnp.where((d >= 0) & (d <= (r & (chunk - 1))), 1.0, 0.0).astype(BF16)
    out = []
    for r0 in range(0, rows, sub):
        hi, mid, lo = _split3(x[r0:r0 + sub])
        dot = lambda t: jnp.dot(tri, t, preferred_element_type=F32)
        out.append(dot(hi) + dot(mid) + dot(lo))
    return out[0] if len(out) == 1 else jnp.concatenate(out, axis=0)


def _proj_gla_kernel(x_ref, lng_ref, lnb_ref, w_ref, wga_ref, wa2_ref, ba_ref,
                     q_ref, k_ref, v_ref, g_ref, b_ref, *, kw, gw, q_scale, chunk):
    h = _layer_norm(x_ref[...], lng_ref[...], lnb_ref[...]).astype(BF16)
    z = lax.dot_general(h, w_ref[...], _NT, preferred_element_type=F32)
    q_ref[...] = z[:, :kw] * q_scale
    k_ref[...] = z[:, kw:2 * kw]
    v_ref[...] = z[:, 2 * kw:2 * kw + gw].astype(BF16)
    g_ref[...] = z[:, 2 * kw + gw:]
    ga = lax.dot_general(h, wga_ref[...], _NT, preferred_element_type=F32).astype(BF16)
    a = jnp.dot(ga, wa2_ref[...], preferred_element_type=F32) + ba_ref[...]
    log_decay = jax.nn.log_sigmoid(a) * (1.0 / GLA_GATE_TAU)
    b_ref[...] = _chunk_cumsum(log_decay, chunk) if chunk > 1 else log_decay


def _proj_gla(x, lng, lnb, w, wga, wa2, ba, *, tm, kw, gw, q_scale, chunk):
    m, d = x.shape
    assert chunk & (chunk - 1) == 0 and (tm % chunk == 0 or chunk % tm == 0)
    row = lambda n: pl.BlockSpec((tm, n), lambda i: (i, 0))
    return pl.pallas_call(
        functools.partial(_proj_gla_kernel, kw=kw, gw=gw, q_scale=q_scale, chunk=chunk),
        grid=(m // tm,),
        in_specs=[row(d), _full(lng.shape), _full(lnb.shape), _full((2 * kw + 2 * gw, d)), _full(wga.shape),
                  _full(wa2.shape), _full(ba.shape)],
        out_specs=[row(kw), row(kw), row(gw), row(gw), row(kw)],
        out_shape=[jax.ShapeDtypeStruct((m, kw), F32), jax.ShapeDtypeStruct((m, kw), F32),
                   jax.ShapeDtypeStruct((m, gw), BF16), jax.ShapeDtypeStruct((m, gw), F32),
                   jax.ShapeDtypeStruct((m, kw), F32)],
        compiler_params=_params("parallel"),
        name="proj_gla",
    )(x, lng, lnb, w, wga, wa2, ba)


def _proj_mla_kernel(x_ref, lng_ref, lnb_ref, w_ref, qg_ref, kvg_ref, wuq_ref, wn_ref, cos_ref, sin_ref,
                     q_ref, ckv_ref, kpe_ref, kcat_ref, *, qr, kvr, rope, heads, nope, scale):
    h = _layer_norm(x_ref[...], lng_ref[...], lnb_ref[...]).astype(BF16)
    z = lax.dot_general(h, w_ref[...], _NT, preferred_element_type=F32)
    cqn = _rms_norm(z[:, :qr], qg_ref[...]).astype(BF16)
    ckvn = _rms_norm(z[:, qr:qr + kvr], kvg_ref[...])
    cos = cos_ref[...]
    sin = sin_ref[...]
    kpe = z[:, qr + kvr:qr + kvr + rope]
    kpe_rot = z[:, qr + kvr + rope:]
    kpe_r = kpe * cos[:, :rope] + kpe_rot * sin[:, :rope]
    ckv_ref[...] = ckvn
    kpe_ref[...] = kpe_r
    kcat_ref[:, :kvr] = ckvn.astype(BF16)
    kcat_ref[:, kvr:] = kpe_r.astype(BF16)
    qf = jnp.dot(cqn, wuq_ref[...], preferred_element_type=F32)
    hn = heads * nope
    hr = heads * rope
    cos_all = jnp.concatenate([cos] * (hr // cos.shape[1]), axis=1)
    sin_all = jnp.concatenate([sin] * (hr // sin.shape[1]), axis=1)
    q_pe = (qf[:, hn:hn + hr] * cos_all + qf[:, hn + hr:] * sin_all) * scale
    for hd in range(heads):
        q_lat = jnp.dot(qf[:, hd * nope:(hd + 1) * nope].astype(BF16), wn_ref[hd],
                        preferred_element_type=F32) * scale
        q_ref[hd, :, :kvr] = q_lat.astype(BF16)
        q_ref[hd, :, kvr:] = q_pe[:, hd * rope:(hd + 1) * rope].astype(BF16)


def _proj_mla(x, lng, lnb, w, qg, kvg, wuq, wn, cos8, sin8, *, tm, rope, scale):
    m, d = x.shape
    heads, nope, kvr = wn.shape
    qr = qg.shape[-1]
    n_pos_blocks = cos8.shape[0] // tm
    row = lambda n: pl.BlockSpec((tm, n), lambda i: (i, 0))
    tab = pl.BlockSpec((tm, cos8.shape[1]), lambda i: (i % n_pos_blocks, 0))
    return pl.pallas_call(
        functools.partial(_proj_mla_kernel, qr=qr, kvr=kvr, rope=rope, heads=heads, nope=nope, scale=scale),
        grid=(m // tm,),
        in_specs=[row(d), _full(lng.shape), _full(lnb.shape), _full(w.shape), _full(qg.shape), _full(kvg.shape),
                  _full(wuq.shape), _full(wn.shape), tab, tab],
        out_specs=[pl.BlockSpec((heads, tm, kvr + rope), lambda i: (0, i, 0)), row(kvr), row(rope), row(kvr + rope)],
        out_shape=[jax.ShapeDtypeStruct((heads, m, kvr + rope), BF16), jax.ShapeDtypeStruct((m, kvr), F32),
                   jax.ShapeDtypeStruct((m, rope), F32), jax.ShapeDtypeStruct((m, kvr + rope), BF16)],
        compiler_params=_params("parallel"),
        name="proj_mla",
    )(x, lng, lnb, w, qg, kvg, wuq, wn, cos8, sin8)


def _mla_prompt_kernel(q_ref, k_ref, kmeta_ref, o_ref, m_sc, l_sc, acc_sc, *, tq, kvr, n_meta):
    i = pl.program_id(2)
    heads, _, dq = q_ref.shape
    rows = heads * tq
    half = tq // 2
    hrows = heads * half
    q = jnp.concatenate([q_ref[:, :half, :].reshape(hrows, dq), q_ref[:, half:, :].reshape(hrows, dq)], axis=0)
    nt = (((1,), (1,)), ((), ()))

    m_sc[...] = jnp.full_like(m_sc, -jnp.inf)
    l_sc[...] = jnp.zeros_like(l_sc)
    acc_sc[...] = jnp.zeros_like(acc_sc)

    def update(s, kblk, rs=slice(None)):
        chunks = [s[:, c * LANES:(c + 1) * LANES] for c in range(s.shape[1] // LANES)]
        m_prev = m_sc[rs, :]
        row_max = jnp.max(functools.reduce(jnp.maximum, chunks), axis=-1, keepdims=True)
        m_new = jnp.maximum(m_prev, jnp.broadcast_to(row_max, m_prev.shape))
        corr = jnp.exp(m_prev - m_new)
        ps = [jnp.exp(c - m_new) for c in chunks]
        l_sc[rs, :] = l_sc[rs, :] * corr + functools.reduce(jnp.add, ps)
        p = jnp.concatenate([x.astype(BF16) for x in ps], axis=1)
        pv = jnp.dot(p, kblk[:, :kvr], preferred_element_type=F32)
        acc_sc[rs, :] = acc_sc[rs, :] * jnp.concatenate([corr] * (kvr // LANES), axis=1) + pv
        m_sc[rs, :] = m_new

    k_lo = k_ref[pl.ds(pl.multiple_of(i * tq, tq), half), :]
    k_hi = k_ref[pl.ds(pl.multiple_of(i * tq + half, half), half), :]
    k_first = jnp.concatenate([kmeta_ref[...], k_lo], axis=0)
    width = LANES + half
    s = lax.dot_general(q, k_first, nt, preferred_element_type=F32)
    tok = lax.broadcasted_iota(jnp.int32, (2, heads, half, width), 2).reshape(rows, width)
    second_half = lax.broadcasted_iota(jnp.int32, (2, hrows, width), 0).reshape(rows, width)
    key = lax.broadcasted_iota(jnp.int32, (rows, width), 1)
    visible = (key < n_meta) | ((key >= LANES) & ((key - LANES <= tok) | (second_half > 0)))
    update(jnp.where(visible, s, -jnp.inf), k_first)

    s = lax.dot_general(q[hrows:], k_hi, nt, preferred_element_type=F32)
    tok = lax.broadcasted_iota(jnp.int32, (heads, half, half), 1).reshape(hrows, half)
    key = lax.broadcasted_iota(jnp.int32, (hrows, half), 1)
    update(jnp.where(key <= tok, s, -jnp.inf), k_hi, slice(hrows, rows))

    def body(j, carry):
        kblk = k_ref[pl.ds(pl.multiple_of(j * tq, tq), tq), :]
        update(lax.dot_general(q, kblk, nt, preferred_element_type=F32), kblk)
        return carry

    lax.fori_loop(0, i, body, 0)

    o = (acc_sc[...] / jnp.sum(l_sc[...], axis=-1, keepdims=True)).astype(o_ref.dtype)
    o_ref[:, :half, :] = o[:hrows].reshape(heads, half, kvr)
    o_ref[:, half:, :] = o[hrows:].reshape(heads, half, kvr)


def _mla_prompt(q, kcat, kmeta, *, bsz, seq, tq, kvr, heads_per_block):
    heads, m, dq = q.shape
    nq = seq // tq
    rows = heads_per_block * tq
    n_meta = kmeta.shape[0]
    assert n_meta <= LANES and (tq // 2) % LANES == 0
    kmeta = jnp.pad(kmeta, ((0, LANES - n_meta), (0, 0)))
    return pl.pallas_call(
        functools.partial(_mla_prompt_kernel, tq=tq, kvr=kvr, n_meta=n_meta),
        grid=(bsz, heads // heads_per_block, nq),
        in_specs=[pl.BlockSpec((heads_per_block, tq, dq), lambda b, h, i: (h, b * nq + i, 0)),
                  pl.BlockSpec((seq, dq), lambda b, h, i: (b, 0)),
                  pl.BlockSpec(kmeta.shape, lambda b, h, i: (0, 0))],
        out_specs=pl.BlockSpec((heads_per_block, tq, kvr), lambda b, h, i: (h, b * nq + i, 0)),
        out_shape=jax.ShapeDtypeStruct((heads, m, kvr), BF16),
        scratch_shapes=[pltpu.VMEM((rows, LANES), F32), pltpu.VMEM((rows, LANES), F32), pltpu.VMEM((rows, kvr), F32)],
        compiler_params=_params("parallel", "parallel", "arbitrary"),
        name="mla_prompt",
    )(q, kcat, kmeta)


def _gla_state_terms(kd, v, decay_last):
    c, dk = kd.shape
    aug = jnp.concatenate([kd, jnp.broadcast_to(decay_last, (8, dk)), jnp.zeros((dk - c - 8, dk), F32)], axis=0)
    aug_t = jnp.transpose(aug)
    return aug_t[:, c:c + 1], jnp.dot(aug_t[:, :c].astype(BF16), v, preferred_element_type=F32)


def _gla_prompt_kernel(q_ref, k_ref, v_ref, b_ref, km_ref, vm_ref, bm_ref, o_ref, sfin_ref, *, chunk):
    seq = q_ref.shape[0]
    n_chunks = seq // chunk

    bm = bm_ref[...]
    bm_last = bm[-1:, :]
    _, s_meta = _gla_state_terms(km_ref[...] * jnp.exp(bm_last - bm), vm_ref[...], jnp.exp(bm_last))

    safe = jnp.max(-b_ref[...]) <= GLA_SAFE_RANGE

    def load(c):
        r0 = pl.multiple_of(c * chunk, chunk)
        rows = pl.ds(r0, chunk)
        return rows, q_ref[rows, :], k_ref[rows, :], v_ref[rows, :], b_ref[rows, :]

    causal = lax.broadcasted_iota(jnp.int32, (chunk, chunk), 1) <= lax.broadcasted_iota(jnp.int32, (chunk, chunk), 0)

    def fast_chunk(c, s):
        rows, q, k, v, b = load(c)
        b_last = b[-1:, :]
        o_inter = jnp.dot((q * jnp.exp(b)).astype(BF16), s.astype(BF16), preferred_element_type=F32)
        kd = k * jnp.exp(b_last - b)
        qd = q * jnp.exp(b - b_last)
        a = lax.dot_general(qd.astype(BF16), kd.astype(BF16), (((1,), (1,)), ((), ())), preferred_element_type=F32)
        a = jnp.where(causal, a, 0.0)
        o_ref[rows, :] = o_inter + jnp.dot(a.astype(BF16), v, preferred_element_type=F32)
        decay_col, kv = _gla_state_terms(kd, v, jnp.exp(b_last))
        return s * decay_col + kv

    def exact_chunk(c, s):
        rows, q, k, v, b = load(c)
        r0 = pl.multiple_of(c * chunk, chunk)
        b_last = b[-1:, :]
        o_ref[rows, :] = jnp.dot((q * jnp.exp(b)).astype(BF16), s.astype(BF16), preferred_element_type=F32)
        vf = v.astype(F32)
        row_id = lax.broadcasted_iota(jnp.int32, (chunk, 1), 0)

        def token(t, carry2):
            sel = row_id == t
            bt = jnp.sum(jnp.where(sel, b, 0.0), axis=0, keepdims=True)
            qt = jnp.sum(jnp.where(sel, q, 0.0), axis=0, keepdims=True)
            w = jnp.exp(jnp.where(row_id <= t, bt - b, -jnp.inf))
            a_col = jnp.sum(w * qt * k, axis=-1, keepdims=True)
            o_t = jnp.sum(a_col * vf, axis=0, keepdims=True)
            o_ref[pl.ds(r0 + t, 1), :] = o_ref[pl.ds(r0 + t, 1), :] + o_t
            return carry2

        lax.fori_loop(0, chunk, token, 0)
        decay_col, kv = _gla_state_terms(k * jnp.exp(b_last - b), v, jnp.exp(b_last))
        return s * decay_col + kv

    @pl.when(safe)
    def _():
        sfin_ref[...] = lax.fori_loop(0, n_chunks, fast_chunk, s_meta, unroll=GLA_UNROLL)

    @pl.when(jnp.logical_not(safe))
    def _():
        sfin_ref[...] = lax.fori_loop(0, n_chunks, exact_chunk, s_meta)


def _gla_prompt(q, k, v, la, km, vm, lam, *, bsz, seq, heads, dk, dv):
    m = q.shape[0]
    nm = km.shape[0]
    col = lambda n: pl.BlockSpec((seq, n), lambda b, h: (b, h))
    mcol = lambda n: pl.BlockSpec((nm, n), lambda b, h: (0, h))
    return pl.pallas_call(
        functools.partial(_gla_prompt_kernel, chunk=GLA_CHUNK),
        grid=(bsz, heads),
        in_specs=[col(dk), col(dk), col(dv), col(dk), mcol(dk), mcol(dv), mcol(dk)],
        out_specs=[pl.BlockSpec((None, seq, dv), lambda b, h: (h, b, 0)),
                   pl.BlockSpec((None, None, dk, dv), lambda b, h: (b, h, 0, 0))],
        out_shape=[jax.ShapeDtypeStruct((heads, m, dv), F32), jax.ShapeDtypeStruct((bsz, heads, dk, dv), F32)],
        compiler_params=_params("parallel", "parallel"),
        name="gla_prompt",
    )(q, k, v, la, km, vm, lam)


def _mla_decode_kernel(pt_ref, q_ref, ckvs_ref, kpes_ref, ckv_hbm, kpe_hbm, o_ref,
                       ckv_buf, kpe_buf, ckv_sem, kpe_sem, *, group, slots, kvr, page):
    nb, n_pages = pt_ref.shape
    steps = n_pages // group
    ahead = slots - 1
    heads = q_ref.shape[1]

    def copies(b, g, slot):
        out = []
        for p in range(group):
            pid = pt_ref[b, g * group + p]
            out.append(pltpu.make_async_copy(ckv_hbm.at[0, pid], ckv_buf.at[slot, pl.ds(p * page, page), :],
                                             ckv_sem.at[slot]))
            out.append(pltpu.make_async_copy(kpe_hbm.at[0, pid], kpe_buf.at[slot, p], kpe_sem.at[slot]))
        return out

    def start(b, g, slot):
        for c in copies(b, g, slot):
            c.start()

    def wait(slot):
        for c in copies(0, 0, slot):
            c.wait()

    def start_ahead(b, g):
        t = g + ahead
        if t < steps:
            start(b, t, t % slots)
        else:
            @pl.when(b + 1 < nb)
            def _():
                start(b + 1, t - steps, (t - steps) % slots)

    b = pl.program_id(0)

    @pl.when(b == 0)
    def _():
        for t in range(ahead):
            start(0, t, t % slots)

    nt = (((1,), (1,)), ((), ()))
    q = q_ref[b]
    ckv_new = ckvs_ref[b]
    kpe_new = kpes_ref[b]
    m = (jnp.sum(q[:, :kvr] * ckv_new, axis=-1, keepdims=True)
         + jnp.sum(q[:, kvr:] * kpe_new, axis=-1, keepdims=True))
    l = jnp.ones_like(m)
    acc = jnp.broadcast_to(ckv_new, (heads, kvr))
    q_lat = q[:, :kvr].astype(BF16)
    q_pe = q[:, kvr:].astype(BF16)
    for g in range(steps):
        slot = g % slots
        wait(slot)
        start_ahead(b, g)
        ck = ckv_buf[slot].astype(BF16)
        kp = jnp.concatenate([kpe_buf[slot, p] for p in range(group)], axis=1).astype(BF16)
        s = (lax.dot_general(q_lat, ck, nt, preferred_element_type=F32)
             + jnp.dot(q_pe, kp, preferred_element_type=F32))
        m_new = jnp.maximum(m, jnp.max(s, axis=-1, keepdims=True))
        corr = jnp.exp(m - m_new)
        p = jnp.exp(s - m_new)
        l = l * corr + jnp.sum(p, axis=-1, keepdims=True)
        acc = acc * corr + jnp.dot(p.astype(BF16), ck, preferred_element_type=F32)
        m = m_new
    o_ref[b] = acc / l


def _decode_group(n_pages, slots):
    group = min(DECODE_GROUP_PAGES, n_pages // slots)
    while n_pages % (slots * group):
        group -= 1
    return group


def _mla_decode(page_table, q, ckv_new, kpe_new, cache_ckv, cache_kpe_t):
    slots = DECODE_RING_SLOTS
    nb, heads, _ = q.shape
    n_pages = page_table.shape[1]
    page, kvr = cache_ckv.shape[2:]
    rope = cache_kpe_t.shape[2]
    group = _decode_group(n_pages, slots)
    vmem = pl.BlockSpec(memory_space=pltpu.VMEM)
    hbm = pl.BlockSpec(memory_space=pl.ANY)
    grid_spec = pltpu.PrefetchScalarGridSpec(
        num_scalar_prefetch=1,
        grid=(nb,),
        in_specs=[vmem, vmem, vmem, hbm, hbm],
        out_specs=vmem,
        scratch_shapes=[pltpu.VMEM((slots, group * page, kvr), F32), pltpu.VMEM((slots, group, rope, page), F32),
                        pltpu.SemaphoreType.DMA((slots,)), pltpu.SemaphoreType.DMA((slots,))],
    )
    return pl.pallas_call(
        functools.partial(_mla_decode_kernel, group=group, slots=slots, kvr=kvr, page=page),
        grid_spec=grid_spec,
        out_shape=jax.ShapeDtypeStruct((nb, heads, kvr), F32),
        compiler_params=_params("arbitrary"),
        name="mla_decode",
    )(page_table, q, ckv_new, kpe_new, cache_ckv, cache_kpe_t)


def _gla_decode_kernel(q_ref, k_ref, v_ref, la_ref, s_ref, o_ref, snew_ref, *, heads, dk, dv):
    tb = q_ref.shape[0]
    decay = jnp.exp(la_ref[...])
    q = q_ref[...]
    k = k_ref[...]
    v = v_ref[...].astype(F32)
    qd = (q * decay).astype(BF16)
    head = lambda a, h, n: a[:, h * n:(h + 1) * n]
    rows = jnp.concatenate([head(decay, h, dk) for h in range(heads)] + [head(k, h, dk) for h in range(heads)]
                           + [jnp.zeros((dk - 2 * heads * tb, dk), F32)], axis=0)
    cols = jnp.transpose(rows)
    for bi in range(tb):
        for h in range(heads):
            s_old = s_ref[bi, h]
            qh = head(q, h, dk)[bi:bi + 1]
            kh = head(k, h, dk)[bi:bi + 1]
            vh = head(v, h, dv)[bi:bi + 1]
            qk = jnp.sum(qh * kh, axis=-1, keepdims=True)
            qd8 = jnp.broadcast_to(head(qd, h, dk)[bi:bi + 1], (8, dk))
            o_h = jnp.dot(qd8, s_old.astype(BF16), preferred_element_type=F32)[:1, :] + qk * vh
            o_ref[bi:bi + 1, h * dv:(h + 1) * dv] = o_h
            c_decay = h * tb + bi
            c_key = (heads + h) * tb + bi
            snew_ref[bi, h] = s_old * cols[:, c_decay:c_decay + 1] + cols[:, c_key:c_key + 1] * vh


def _gla_decode(q, k, v, la, state, *, tb):
    nb, heads, dk, dv = state.shape
    row = lambda n: pl.BlockSpec((tb, n), lambda i: (i, 0))
    st = pl.BlockSpec((tb, heads, dk, dv), lambda i: (i, 0, 0, 0))
    return pl.pallas_call(
        functools.partial(_gla_decode_kernel, heads=heads, dk=dk, dv=dv),
        grid=(nb // tb,),
        in_specs=[row(heads * dk), row(heads * dk), row(heads * dv), row(heads * dk), st],
        out_specs=[row(heads * dv), st],
        out_shape=[jax.ShapeDtypeStruct((nb, heads * dv), F32), jax.ShapeDtypeStruct(state.shape, F32)],
        compiler_params=_params("parallel"),
        name="gla_decode",
    )(q, k, v, la, state)


def _mix_out_kernel(x_ref, lng_ref, lnb_ref, og_ref, g_ref, ol_ref, ng_ref, wv_ref, wo_ref, l1g_ref, l1b_ref,
                    h1_ref, mix_sc, *, alpha):
    gla_heads, _, dv = og_ref.shape
    mla_heads, _, vd = wv_ref.shape
    gw = gla_heads * dv
    gate = g_ref[...]
    for h in range(gla_heads):
        og = _rms_norm(og_ref[h], ng_ref[...])
        gh = gate[:, h * dv:(h + 1) * dv]
        mix_sc[:, h * dv:(h + 1) * dv] = (og * (gh * jax.nn.sigmoid(gh))).astype(BF16)
    for h in range(mla_heads):
        om = jnp.dot(ol_ref[h], wv_ref[h], preferred_element_type=F32)
        mix_sc[:, gw + h * vd:gw + (h + 1) * vd] = om.astype(BF16)
    mix = jnp.dot(mix_sc[...], wo_ref[...], preferred_element_type=F32)
    hres = _layer_norm(x_ref[...], lng_ref[...], lnb_ref[...])
    h1_ref[...] = _layer_norm(alpha * hres + mix, l1g_ref[...], l1b_ref[...])


def _mix_out(x, lng, lnb, og, gate, ol, ng, wv, wo, l1g, l1b, *, tm, alpha):
    m, d = x.shape
    gla_heads, _, dv = og.shape
    mla_heads, _, kvr = ol.shape
    row = lambda n: pl.BlockSpec((tm, n), lambda i: (i, 0))
    return pl.pallas_call(
        functools.partial(_mix_out_kernel, alpha=alpha),
        grid=(m // tm,),
        in_specs=[row(d), _full(lng.shape), _full(lnb.shape),
                  pl.BlockSpec((gla_heads, tm, dv), lambda i: (0, i, 0)), row(gate.shape[1]),
                  pl.BlockSpec((mla_heads, tm, kvr), lambda i: (0, i, 0)),
                  _full(ng.shape), _full(wv.shape), _full(wo.shape), _full(l1g.shape), _full(l1b.shape)],
        out_specs=row(d),
        out_shape=jax.ShapeDtypeStruct((m, d), F32),
        scratch_shapes=[pltpu.VMEM((tm, wo.shape[0]), BF16)],
        compiler_params=_params("parallel"),
        name="mix_out",
    )(x, lng, lnb, og, gate, ol, ng, wv, wo, l1g, l1b)


def _mlp_kernel(h_ref, wu_ref, wd_ref, l2g_ref, l2b_ref, o_ref, *rest, alpha):
    *w_bf16_refs, hb_sc = rest
    j = pl.program_id(1)

    @pl.when(j == 0)
    def _():
        hb_sc[...] = h_ref[...].astype(BF16)
        o_ref[...] = jnp.zeros_like(o_ref)

    wu, wd = wu_ref[...], wd_ref[...]
    if w_bf16_refs:
        wu, wd = wu.astype(BF16), wd.astype(BF16)
        w_bf16_refs[0][...] = wu
        w_bf16_refs[1][...] = wd
    u = jnp.maximum(jnp.dot(hb_sc[...], wu, preferred_element_type=F32), 0.0)
    o_ref[...] += jnp.dot((u * u).astype(BF16), wd, preferred_element_type=F32)

    @pl.when(j == pl.num_programs(1) - 1)
    def _():
        o_ref[...] = _layer_norm(alpha * h_ref[...] + o_ref[...], l2g_ref[...], l2b_ref[...])


def _mlp(h1, wu, wd, l2g, l2b, *, tm, tf, alpha):
    m, d = h1.shape
    ff = wu.shape[1]
    emit = wu.dtype != BF16
    assert not emit or m == tm
    wu_spec = pl.BlockSpec((d, tf), lambda i, j: (0, j))
    wd_spec = pl.BlockSpec((tf, d), lambda i, j: (j, 0))
    y_spec = pl.BlockSpec((tm, d), lambda i, j: (i, 0))
    y_shape = jax.ShapeDtypeStruct((m, d), F32)
    return pl.pallas_call(
        functools.partial(_mlp_kernel, alpha=alpha),
        grid=(m // tm, ff // tf),
        in_specs=[pl.BlockSpec((tm, d), lambda i, j: (i, 0)), wu_spec, wd_spec, _full(l2g.shape), _full(l2b.shape)],
        out_specs=[y_spec, wu_spec, wd_spec] if emit else y_spec,
        out_shape=([y_shape, jax.ShapeDtypeStruct(wu.shape, BF16), jax.ShapeDtypeStruct(wd.shape, BF16)]
                   if emit else y_shape),
        scratch_shapes=[pltpu.VMEM((tm, d), BF16)],
        compiler_params=_params("parallel", "arbitrary"),
        name="mlp",
    )(h1, wu, wd, l2g, l2b)


def _rope_tables(pos, rope):
    half = rope // 2
    inv_freq = ROPE_BASE ** (-jnp.arange(half, dtype=F32) / half)
    ang = pos.astype(F32)[:, None] * inv_freq[None, :]
    reps = LANES // half
    return jnp.tile(jnp.cos(ang), (1, reps)), jnp.tile(jnp.sin(ang), (1, reps))


def _rotate_half_cols(w):
    half = w.shape[-1] // 2
    return jnp.concatenate([-w[..., half:], w[..., :half]], axis=-1)


def _row_tile(m, cap):
    t = min(m, cap)
    while m % t:
        t //= 2
    return t


def kernel(x_prompt, x_sample, cache_ckv, cache_kpe, state_gla, page_table, meta_tokens, ln_emb_g, ln_emb_b, w_in, w_gla_a2, b_gla_a, gla_norm_g, mla_q_norm_g, w_mla_uq, mla_kv_norm_g, w_mla_ukv, w_out, ln1_g, ln1_b, w_up, w_down, ln2_g, ln2_b):
    bsz, seq, d = x_prompt.shape
    nb, dec_seq, _ = x_sample.shape
    depth = w_in.shape[0]
    assert depth == 1 and dec_seq == 1
    _, gla_heads, dk, dv = state_gla.shape[1:]
    page, kvr = cache_ckv.shape[2:]
    rope = cache_kpe.shape[3]
    n_pages = page_table.shape[1]
    n_meta = meta_tokens.shape[0]
    rank = w_gla_a2.shape[1]
    kw, gw = gla_heads * dk, gla_heads * dv
    qr = mla_q_norm_g.shape[-1]
    mla_heads = w_mla_ukv.shape[2]
    nope = w_mla_uq.shape[-1] // mla_heads - rope
    alpha = (2 * depth) ** 0.25
    mla_scale = (nope + rope) ** -0.5
    assert rank <= LANES and seq % GLA_CHUNK == 0 and n_pages % 2 == 0

    w0t = jnp.swapaxes(w_in[0], 0, 1).astype(BF16)
    o_ga = 2 * kw + 2 * gw
    o_cq = o_ga + rank
    o_kpe = o_cq + qr + kvr
    w_ga = jnp.pad(w0t[o_ga:o_cq], ((0, LANES - rank), (0, 0)))
    w_a2 = jnp.pad(w_gla_a2[0], ((0, LANES - rank), (0, 0))).astype(BF16)
    b_a = b_gla_a[0][None, :]
    w_mla = jnp.concatenate([w0t[o_cq:], _rotate_half_cols(w0t[o_kpe:].T).T], axis=0)
    uq = w_mla_uq[0].reshape(qr, mla_heads, nope + rope)
    uq_rope = uq[..., nope:]
    w_uq = jnp.concatenate([uq[..., :nope].reshape(qr, mla_heads * nope), uq_rope.reshape(qr, mla_heads * rope),
                            _rotate_half_cols(uq_rope).reshape(qr, mla_heads * rope)], axis=1).astype(BF16)
    ukv = w_mla_ukv[0]
    w_n = jnp.transpose(ukv[..., :nope], (1, 2, 0)).astype(BF16)
    w_v = jnp.transpose(ukv[..., nope:], (1, 0, 2)).astype(BF16)
    w_o = w_out[0].astype(BF16)
    lng, lnb = ln_emb_g[None, :], ln_emb_b[None, :]
    qg, kvg, ng = mla_q_norm_g, mla_kv_norm_g, gla_norm_g
    l1g, l1b, l2g, l2b = ln1_g, ln1_b, ln2_g, ln2_b

    assert LANES % rope == 0 and (mla_heads * rope) % LANES == 0
    cos_m, sin_m = _rope_tables(jnp.arange(n_meta, dtype=jnp.int32), rope)
    cos_p, sin_p = _rope_tables(n_meta + jnp.arange(seq, dtype=jnp.int32), rope)
    cos_s, sin_s = _rope_tables(jnp.full((nb,), n_pages * page, dtype=jnp.int32), rope)

    def project(x2d, cos8, sin8, tm, chunk):
        gla = _proj_gla(x2d, lng, lnb, w0t, w_ga, w_a2, b_a, tm=tm, kw=kw, gw=gw, q_scale=dk ** -0.5, chunk=chunk)
        mla = _proj_mla(x2d, lng, lnb, w_mla, qg, kvg, w_uq, w_n, cos8, sin8, tm=tm, rope=rope, scale=mla_scale)
        return gla, mla

    def merge(x2d, o_gla, gate, o_lat, tm):
        return _mix_out(x2d, lng, lnb, o_gla, gate, o_lat, ng, w_v, w_o, l1g, l1b, tm=tm, alpha=alpha)

    xp = x_prompt.reshape(bsz * seq, d)
    (_, km, vm, _, bm), (_, ckv_m, kpe_m, kcat_m) = project(meta_tokens, cos_m, sin_m, n_meta, GLA_CHUNK)
    tm_p = _row_tile(seq, PROJ_ROW_TILE)
    (qg_p, kg_p, vg_p, gate_p, b_p), (q_p, ckv_p, kpe_p, kcat_p) = project(xp, cos_p, sin_p, tm_p, GLA_CHUNK)
    o_gla_p, s_p = _gla_prompt(qg_p, kg_p, vg_p, b_p, km, vm, bm,
                               bsz=bsz, seq=seq, heads=gla_heads, dk=dk, dv=dv)
    o_lat_p = _mla_prompt(q_p, kcat_p, kcat_m, bsz=bsz, seq=seq, tq=_row_tile(seq, MLA_PROMPT_BLOCK), kvr=kvr,
                          heads_per_block=min(mla_heads, MLA_PROMPT_HEADS_PER_BLOCK))
    h1_p = merge(xp, o_gla_p, gate_p, o_lat_p, tm_p)

    xs = x_sample.reshape(nb, d)
    (qg_s, kg_s, vg_s, gate_s, la_s), (q_s, ckv_s, kpe_s, _) = project(xs, cos_s, sin_s, nb, 1)
    o_gla_s, s_s = _gla_decode(qg_s, kg_s, vg_s, la_s, state_gla[0], tb=_row_tile(nb, 8))
    q_dec = jnp.transpose(q_s, (1, 0, 2)).astype(F32)
    cache_kpe_t = jnp.swapaxes(cache_kpe, 2, 3)
    o_lat_s = _mla_decode(page_table, q_dec, ckv_s[:, None, :], kpe_s[:, None, :], cache_ckv, cache_kpe_t)
    o_gla_s = jnp.transpose(o_gla_s.reshape(nb, gla_heads, dv), (1, 0, 2))
    o_lat_s = jnp.transpose(o_lat_s, (1, 0, 2)).astype(BF16)
    h1_s = merge(xs, o_gla_s, gate_s, o_lat_s, nb)

    y_s, w_u, w_d = _mlp(h1_s, w_up[0], w_down[0], l2g, l2b, tm=nb, tf=MLP_FF_TILE, alpha=alpha)
    y_p = _mlp(h1_p, w_u, w_d, l2g, l2b, tm=_row_tile(bsz * seq, MLP_ROW_TILE), tf=MLP_FF_TILE, alpha=alpha)

    def with_meta(meta_rows, rows):
        meta_b = jnp.broadcast_to(meta_rows[None], (bsz,) + meta_rows.shape)
        return jnp.concatenate([meta_b, rows.reshape(bsz, seq, rows.shape[-1])], axis=1)[None]

    return (y_p.reshape(bsz, seq, d), y_s.reshape(nb, 1, d),
            with_meta(ckv_m, ckv_p), with_meta(kpe_m, kpe_p), s_p[None],
            ckv_s.reshape(1, nb, 1, kvr), kpe_s.reshape(1, nb, 1, rope), s_s[None])
```

```python
import functools

import jax
import jax.numpy as jnp
from jax import lax
from jax.experimental import pallas as pl
from jax.experimental.pallas import tpu as pltpu

F32 = jnp.float32
BF16 = jnp.bfloat16

LN_EPS = 1e-5
RMS_EPS = 1e-6
ROPE_BASE = 10000.0
GLA_GATE_TAU = 16.0
GLA_CHUNK = 64
GLA_SAFE_RANGE = 60.0
GLA_UNROLL = 8
LANES = 128
VMEM_LIMIT_BYTES = 56 * 1024 * 1024
PROJ_VMEM_LIMIT_BYTES = 60 * 1024 * 1024
DECODE_GROUP_PAGES = 32
DECODE_RING_SLOTS = 4
MLA_PROMPT_BLOCK = 512
MLA_PROMPT_HEADS_PER_BLOCK = 4
PROJ_ROW_TILE = 512
MLP_ROW_TILE = 1024
MLP_FF_TILE = 512


_NT = (((1,), (1,)), ((), ()))


def _params(*sem):
    return pltpu.CompilerParams(dimension_semantics=sem, vmem_limit_bytes=VMEM_LIMIT_BYTES)


def _layer_norm(x, g, b):
    mu = jnp.mean(x, axis=-1, keepdims=True)
    xc = x - mu
    var = jnp.mean(xc * xc, axis=-1, keepdims=True)
    return xc * lax.rsqrt(var + LN_EPS) * g + b


def _rms_norm(x, g):
    return x * lax.rsqrt(jnp.mean(x * x, axis=-1, keepdims=True) + RMS_EPS) * g


def _full(shape):
    return pl.BlockSpec(shape, lambda *_: (0,) * len(shape), pipeline_mode=pl.Buffered(1))


def _split3(x):
    hi = x.astype(BF16)
    r1 = x - hi.astype(F32)
    mid = r1.astype(BF16)
    lo = (r1 - mid.astype(F32)).astype(BF16)
    return hi, mid, lo


def _chunk_cumsum(x, chunk):
    rows = x.shape[0]
    sub = min(rows, max(chunk, LANES))
    r = lax.broadcasted_iota(jnp.int32, (sub, sub), 0)
    d = r - lax.broadcasted_iota(jnp.int32, (sub, sub), 1)
    tri = jnp.where((d >= 0) & (d <= (r & (chunk - 1))), 1.0, 0.0).astype(BF16)
    out = []
    for r0 in range(0, rows, sub):
        hi, mid, lo = _split3(x[r0:r0 + sub])
        dot = lambda t: jnp.dot(tri, t, preferred_element_type=F32)
        out.append(dot(hi) + dot(mid) + dot(lo))
    return out[0] if len(out) == 1 else jnp.concatenate(out, axis=0)


def _proj_gla_body(h, w_ref, wga_ref, wa2_ref, ba_ref, q_ref, k_ref, v_ref, g_ref, b_ref, *, kw, gw, q_scale, chunk):
    z = lax.dot_general(h, w_ref[...], _NT, preferred_element_type=F32)
    q_ref[...] = z[:, :kw] * q_scale
    k_ref[...] = z[:, kw:2 * kw]
    v_ref[...] = z[:, 2 * kw:2 * kw + gw].astype(BF16)
    g_ref[...] = z[:, 2 * kw + gw:]
    ga = lax.dot_general(h, wga_ref[...], _NT, preferred_element_type=F32).astype(BF16)
    a = jnp.dot(ga, wa2_ref[...], preferred_element_type=F32) + ba_ref[...]
    log_decay = jax.nn.log_sigmoid(a) * (1.0 / GLA_GATE_TAU)
    b_ref[...] = _chunk_cumsum(log_decay, chunk) if chunk > 1 else log_decay


def _proj_mla_body(h, w_ref, qg_ref, kvg_ref, wuq_ref, wn_ref, cos_ref, sin_ref,
                   q_ref, ckv_ref, kpe_ref, kcat_ref, *, qr, kvr, rope, heads, nope, scale):
    z = lax.dot_general(h, w_ref[...], _NT, preferred_element_type=F32)
    cqn = _rms_norm(z[:, :qr], qg_ref[...]).astype(BF16)
    ckvn = _rms_norm(z[:, qr:qr + kvr], kvg_ref[...])
    cos = cos_ref[...]
    sin = sin_ref[...]
    kpe = z[:, qr + kvr:qr + kvr + rope]
    kpe_rot = z[:, qr + kvr + rope:]
    kpe_r = kpe * cos[:, :rope] + kpe_rot * sin[:, :rope]
    ckv_ref[...] = ckvn
    kpe_ref[...] = kpe_r
    kcat_ref[:, :kvr] = ckvn.astype(BF16)
    kcat_ref[:, kvr:] = kpe_r.astype(BF16)
    qf = jnp.dot(cqn, wuq_ref[...], preferred_element_type=F32)
    hn = heads * nope
    hr = heads * rope
    cos_all = jnp.concatenate([cos] * (hr // cos.shape[1]), axis=1)
    sin_all = jnp.concatenate([sin] * (hr // sin.shape[1]), axis=1)
    q_pe = (qf[:, hn:hn + hr] * cos_all + qf[:, hn + hr:] * sin_all) * scale
    for hd in range(heads):
        q_lat = jnp.dot(qf[:, hd * nope:(hd + 1) * nope].astype(BF16), wn_ref[hd],
                        preferred_element_type=F32) * scale
        q_ref[hd, :, :kvr] = q_lat.astype(BF16)
        q_ref[hd, :, kvr:] = q_pe[:, hd * rope:(hd + 1) * rope].astype(BF16)


def _proj_kernel(x_ref, lng_ref, lnb_ref, wg_ref, wga_ref, wa2_ref, ba_ref, wm_ref, qg_ref, kvg_ref, wuq_ref, wn_ref,
                 cos_ref, sin_ref, gq_ref, gk_ref, gv_ref, gg_ref, gb_ref, q_ref, ckv_ref, kpe_ref, kcat_ref, *,
                 gla, mla):
    h = _layer_norm(x_ref[...], lng_ref[...], lnb_ref[...]).astype(BF16)
    _proj_gla_body(h, wg_ref, wga_ref, wa2_ref, ba_ref, gq_ref, gk_ref, gv_ref, gg_ref, gb_ref, **gla)
    _proj_mla_body(h, wm_ref, qg_ref, kvg_ref, wuq_ref, wn_ref, cos_ref, sin_ref, q_ref, ckv_ref, kpe_ref, kcat_ref,
                   **mla)


def _proj(x, lng, lnb, w, wga, wa2, ba, wm, qg, kvg, wuq, wn, cos8, sin8, *, tm, kw, gw, q_scale, chunk, rope, scale):
    m, d = x.shape
    heads, nope, kvr = wn.shape
    qr = qg.shape[-1]
    assert chunk & (chunk - 1) == 0 and (tm % chunk == 0 or chunk % tm == 0)
    n_pos_blocks = cos8.shape[0] // tm
    row = lambda n: pl.BlockSpec((tm, n), lambda i: (i, 0))
    tab = pl.BlockSpec((tm, cos8.shape[1]), lambda i: (i % n_pos_blocks, 0))
    sds = jax.ShapeDtypeStruct
    outs = pl.pallas_call(
        functools.partial(_proj_kernel, gla=dict(kw=kw, gw=gw, q_scale=q_scale, chunk=chunk),
                          mla=dict(qr=qr, kvr=kvr, rope=rope, heads=heads, nope=nope, scale=scale)),
        grid=(m // tm,),
        in_specs=[row(d), _full(lng.shape), _full(lnb.shape), _full((2 * kw + 2 * gw, d)), _full(wga.shape),
                  _full(wa2.shape), _full(ba.shape), _full(wm.shape), _full(qg.shape), _full(kvg.shape),
                  _full(wuq.shape), _full(wn.shape), tab, tab],
        out_specs=[row(kw), row(kw), row(gw), row(gw), row(kw),
                   pl.BlockSpec((heads, tm, kvr + rope), lambda i: (0, i, 0)), row(kvr), row(rope), row(kvr + rope)],
        out_shape=[sds((m, kw), F32), sds((m, kw), F32), sds((m, gw), BF16), sds((m, gw), F32), sds((m, kw), F32),
                   sds((heads, m, kvr + rope), BF16), sds((m, kvr), F32), sds((m, rope), F32),
                   sds((m, kvr + rope), BF16)],
        compiler_params=pltpu.CompilerParams(dimension_semantics=("parallel",), vmem_limit_bytes=PROJ_VMEM_LIMIT_BYTES),
        name="proj",
    )(x, lng, lnb, w, wga, wa2, ba, wm, qg, kvg, wuq, wn, cos8, sin8)
    return outs[:5], outs[5:]


def _mla_prompt_kernel(q_ref, k_ref, kmeta_ref, o_ref, m_sc, l_sc, acc_sc, *, tq, kvr, n_meta):
    i = pl.program_id(2)
    heads, _, dq = q_ref.shape
    rows = heads * tq
    half = tq // 2
    hrows = heads * half
    q = jnp.concatenate([q_ref[:, :half, :].reshape(hrows, dq), q_ref[:, half:, :].reshape(hrows, dq)], axis=0)
    nt = (((1,), (1,)), ((), ()))

    m_sc[...] = jnp.full_like(m_sc, -jnp.inf)
    l_sc[...] = jnp.zeros_like(l_sc)
    acc_sc[...] = jnp.zeros_like(acc_sc)

    def update(s, kblk, rs=slice(None)):
        chunks = [s[:, c * LANES:(c + 1) * LANES] for c in range(s.shape[1] // LANES)]
        m_prev = m_sc[rs, :]
        row_max = jnp.max(functools.reduce(jnp.maximum, chunks), axis=-1, keepdims=True)
        m_new = jnp.maximum(m_prev, jnp.broadcast_to(row_max, m_prev.shape))
        corr = jnp.exp(m_prev - m_new)
        ps = [jnp.exp(c - m_new) for c in chunks]
        l_sc[rs, :] = l_sc[rs, :] * corr + functools.reduce(jnp.add, ps)
        p = jnp.concatenate([x.astype(BF16) for x in ps], axis=1)
        pv = jnp.dot(p, kblk[:, :kvr], preferred_element_type=F32)
        acc_sc[rs, :] = acc_sc[rs, :] * jnp.concatenate([corr] * (kvr // LANES), axis=1) + pv
        m_sc[rs, :] = m_new

    k_lo = k_ref[pl.ds(pl.multiple_of(i * tq, tq), half), :]
    k_hi = k_ref[pl.ds(pl.multiple_of(i * tq + half, half), half), :]
    k_first = jnp.concatenate([kmeta_ref[...], k_lo], axis=0)
    width = LANES + half
    s = lax.dot_general(q, k_first, nt, preferred_element_type=F32)
    tok = lax.broadcasted_iota(jnp.int32, (2, heads, half, width), 2).reshape(rows, width)
    second_half = lax.broadcasted_iota(jnp.int32, (2, hrows, width), 0).reshape(rows, width)
    key = lax.broadcasted_iota(jnp.int32, (rows, width), 1)
    visible = (key < n_meta) | ((key >= LANES) & ((key - LANES <= tok) | (second_half > 0)))
    update(jnp.where(visible, s, -jnp.inf), k_first)

    s = lax.dot_general(q[hrows:], k_hi, nt, preferred_element_type=F32)
    tok = lax.broadcasted_iota(jnp.int32, (heads, half, half), 1).reshape(hrows, half)
    key = lax.broadcasted_iota(jnp.int32, (hrows, half), 1)
    update(jnp.where(key <= tok, s, -jnp.inf), k_hi, slice(hrows, rows))

    def body(j, carry):
        kblk = k_ref[pl.ds(pl.multiple_of(j * tq, tq), tq), :]
        update(lax.dot_general(q, kblk, nt, preferred_element_type=F32), kblk)
        return carry

    lax.fori_loop(0, i, body, 0)

    o = (acc_sc[...] / jnp.sum(l_sc[...], axis=-1, keepdims=True)).astype(o_ref.dtype)
    o_ref[:, :half, :] = o[:hrows].reshape(heads, half, kvr)
    o_ref[:, half:, :] = o[hrows:].reshape(heads, half, kvr)


def _mla_prompt(q, kcat, kmeta, *, bsz, seq, tq, kvr, heads_per_block):
    heads, m, dq = q.shape
    nq = seq // tq
    rows = heads_per_block * tq
    n_meta = kmeta.shape[0]
    assert n_meta <= LANES and (tq // 2) % LANES == 0
    kmeta = jnp.pad(kmeta, ((0, LANES - n_meta), (0, 0)))
    return pl.pallas_call(
        functools.partial(_mla_prompt_kernel, tq=tq, kvr=kvr, n_meta=n_meta),
        grid=(bsz, heads // heads_per_block, nq),
        in_specs=[pl.BlockSpec((heads_per_block, tq, dq), lambda b, h, i: (h, b * nq + i, 0)),
                  pl.BlockSpec((seq, dq), lambda b, h, i: (b, 0)),
                  pl.BlockSpec(kmeta.shape, lambda b, h, i: (0, 0))],
        out_specs=pl.BlockSpec((heads_per_block, tq, kvr), lambda b, h, i: (h, b * nq + i, 0)),
        out_shape=jax.ShapeDtypeStruct((heads, m, kvr), BF16),
        scratch_shapes=[pltpu.VMEM((rows, LANES), F32), pltpu.VMEM((rows, LANES), F32), pltpu.VMEM((rows, kvr), F32)],
        compiler_params=_params("parallel", "parallel", "arbitrary"),
        name="mla_prompt",
    )(q, kcat, kmeta)


def _gla_state_terms(kd, v, decay_last):
    c, dk = kd.shape
    aug = jnp.concatenate([kd, jnp.broadcast_to(decay_last, (8, dk)), jnp.zeros((dk - c - 8, dk), F32)], axis=0)
    aug_t = jnp.transpose(aug)
    return aug_t[:, c:c + 1], jnp.dot(aug_t[:, :c].astype(BF16), v, preferred_element_type=F32)


def _gla_prompt_kernel(q_ref, k_ref, v_ref, b_ref, km_ref, vm_ref, bm_ref, o_ref, sfin_ref, *, chunk):
    seq = q_ref.shape[0]
    n_chunks = seq // chunk

    bm = bm_ref[...]
    bm_last = bm[-1:, :]
    _, s_meta = _gla_state_terms(km_ref[...] * jnp.exp(bm_last - bm), vm_ref[...], jnp.exp(bm_last))

    safe = jnp.max(-b_ref[...]) <= GLA_SAFE_RANGE

    def load(c):
        r0 = pl.multiple_of(c * chunk, chunk)
        rows = pl.ds(r0, chunk)
        return rows, q_ref[rows, :], k_ref[rows, :], v_ref[rows, :], b_ref[rows, :]

    causal = lax.broadcasted_iota(jnp.int32, (chunk, chunk), 1) <= lax.broadcasted_iota(jnp.int32, (chunk, chunk), 0)

    def fast_chunk(c, s):
        rows, q, k, v, b = load(c)
        b_last = b[-1:, :]
        o_inter = jnp.dot((q * jnp.exp(b)).astype(BF16), s.astype(BF16), preferred_element_type=F32)
        kd = k * jnp.exp(b_last - b)
        qd = q * jnp.exp(b - b_last)
        a = lax.dot_general(qd.astype(BF16), kd.astype(BF16), (((1,), (1,)), ((), ())), preferred_element_type=F32)
        a = jnp.where(causal, a, 0.0)
        o_ref[rows, :] = o_inter + jnp.dot(a.astype(BF16), v, preferred_element_type=F32)
        decay_col, kv = _gla_state_terms(kd, v, jnp.exp(b_last))
        return s * decay_col + kv

    def exact_chunk(c, s):
        rows, q, k, v, b = load(c)
        r0 = pl.multiple_of(c * chunk, chunk)
        b_last = b[-1:, :]
        o_ref[rows, :] = jnp.dot((q * jnp.exp(b)).astype(BF16), s.astype(BF16), preferred_element_type=F32)
        vf = v.astype(F32)
        row_id = lax.broadcasted_iota(jnp.int32, (chunk, 1), 0)

        def token(t, carry2):
            sel = row_id == t
            bt = jnp.sum(jnp.where(sel, b, 0.0), axis=0, keepdims=True)
            qt = jnp.sum(jnp.where(sel, q, 0.0), axis=0, keepdims=True)
            w = jnp.exp(jnp.where(row_id <= t, bt - b, -jnp.inf))
            a_col = jnp.sum(w * qt * k, axis=-1, keepdims=True)
            o_t = jnp.sum(a_col * vf, axis=0, keepdims=True)
            o_ref[pl.ds(r0 + t, 1), :] = o_ref[pl.ds(r0 + t, 1), :] + o_t
            return carry2

        lax.fori_loop(0, chunk, token, 0)
        decay_col, kv = _gla_state_terms(k * jnp.exp(b_last - b), v, jnp.exp(b_last))
        return s * decay_col + kv

    @pl.when(safe)
    def _():
        sfin_ref[...] = lax.fori_loop(0, n_chunks, fast_chunk, s_meta, unroll=GLA_UNROLL)

    @pl.when(jnp.logical_not(safe))
    def _():
        sfin_ref[...] = lax.fori_loop(0, n_chunks, exact_chunk, s_meta)


def _gla_prompt(q, k, v, la, km, vm, lam, *, bsz, seq, heads, dk, dv):
    m = q.shape[0]
    nm = km.shape[0]
    col = lambda n: pl.BlockSpec((seq, n), lambda b, h: (b, h))
    mcol = lambda n: pl.BlockSpec((nm, n), lambda b, h: (0, h))
    return pl.pallas_call(
        functools.partial(_gla_prompt_kernel, chunk=GLA_CHUNK),
        grid=(bsz, heads),
        in_specs=[col(dk), col(dk), col(dv), col(dk), mcol(dk), mcol(dv), mcol(dk)],
        out_specs=[pl.BlockSpec((None, seq, dv), lambda b, h: (h, b, 0)),
                   pl.BlockSpec((None, None, dk, dv), lambda b, h: (b, h, 0, 0))],
        out_shape=[jax.ShapeDtypeStruct((heads, m, dv), F32), jax.ShapeDtypeStruct((bsz, heads, dk, dv), F32)],
        compiler_params=_params("parallel", "parallel"),
        name="gla_prompt",
    )(q, k, v, la, km, vm, lam)


def _mla_decode_kernel(pt_ref, q_ref, ckvs_ref, kpes_ref, ckv_hbm, kpe_hbm, o_ref,
                       ckv_buf, kpe_buf, ckv_sem, kpe_sem, *, group, slots, kvr, page):
    nb, n_pages = pt_ref.shape
    steps = n_pages // group
    ahead = slots - 1
    heads = q_ref.shape[1]

    def copies(b, g, slot):
        out = []
        for p in range(group):
            pid = pt_ref[b, g * group + p]
            out.append(pltpu.make_async_copy(ckv_hbm.at[0, pid], ckv_buf.at[slot, pl.ds(p * page, page), :],
                                             ckv_sem.at[slot]))
            out.append(pltpu.make_async_copy(kpe_hbm.at[0, pid], kpe_buf.at[slot, p], kpe_sem.at[slot]))
        return out

    def start(b, g, slot):
        for c in copies(b, g, slot):
            c.start()

    def wait(slot):
        for c in copies(0, 0, slot):
            c.wait()

    def start_ahead(b, g):
        t = g + ahead
        if t < steps:
            start(b, t, t % slots)
        else:
            @pl.when(b + 1 < nb)
            def _():
                start(b + 1, t - steps, (t - steps) % slots)

    b = pl.program_id(0)

    @pl.when(b == 0)
    def _():
        for t in range(ahead):
            start(0, t, t % slots)

    nt = (((1,), (1,)), ((), ()))
    q = q_ref[b]
    ckv_new = ckvs_ref[b]
    kpe_new = kpes_ref[b]
    m = (jnp.sum(q[:, :kvr] * ckv_new, axis=-1, keepdims=True)
         + jnp.sum(q[:, kvr:] * kpe_new, axis=-1, keepdims=True))
    l = jnp.ones_like(m)
    acc = jnp.broadcast_to(ckv_new, (heads, kvr))
    q_lat = q[:, :kvr].astype(BF16)
    q_pe = q[:, kvr:].astype(BF16)
    for g in range(steps):
        slot = g % slots
        wait(slot)
        start_ahead(b, g)
        ck = ckv_buf[slot].astype(BF16)
        kp = jnp.concatenate([kpe_buf[slot, p] for p in range(group)], axis=1).astype(BF16)
        s = (lax.dot_general(q_lat, ck, nt, preferred_element_type=F32)
             + jnp.dot(q_pe, kp, preferred_element_type=F32))
        m_new = jnp.maximum(m, jnp.max(s, axis=-1, keepdims=True))
        corr = jnp.exp(m - m_new)
        p = jnp.exp(s - m_new)
        l = l * corr + jnp.sum(p, axis=-1, keepdims=True)
        acc = acc * corr + jnp.dot(p.astype(BF16), ck, preferred_element_type=F32)
        m = m_new
    o_ref[b] = acc / l


def _decode_group(n_pages, slots):
    group = min(DECODE_GROUP_PAGES, n_pages // slots)
    while n_pages % (slots * group):
        group -= 1
    return group


def _mla_decode(page_table, q, ckv_new, kpe_new, cache_ckv, cache_kpe_t):
    slots = DECODE_RING_SLOTS
    nb, heads, _ = q.shape
    n_pages = page_table.shape[1]
    page, kvr = cache_ckv.shape[2:]
    rope = cache_kpe_t.shape[2]
    group = _decode_group(n_pages, slots)
    vmem = pl.BlockSpec(memory_space=pltpu.VMEM)
    hbm = pl.BlockSpec(memory_space=pl.ANY)
    grid_spec = pltpu.PrefetchScalarGridSpec(
        num_scalar_prefetch=1,
        grid=(nb,),
        in_specs=[vmem, vmem, vmem, hbm, hbm],
        out_specs=vmem,
        scratch_shapes=[pltpu.VMEM((slots, group * page, kvr), F32), pltpu.VMEM((slots, group, rope, page), F32),
                        pltpu.SemaphoreType.DMA((slots,)), pltpu.SemaphoreType.DMA((slots,))],
    )
    return pl.pallas_call(
        functools.partial(_mla_decode_kernel, group=group, slots=slots, kvr=kvr, page=page),
        grid_spec=grid_spec,
        out_shape=jax.ShapeDtypeStruct((nb, heads, kvr), F32),
        compiler_params=_params("arbitrary"),
        name="mla_decode",
    )(page_table, q, ckv_new, kpe_new, cache_ckv, cache_kpe_t)


def _gla_decode_kernel(q_ref, k_ref, v_ref, la_ref, s_ref, o_ref, snew_ref, *, heads, dk, dv):
    tb = q_ref.shape[0]
    decay = jnp.exp(la_ref[...])
    q = q_ref[...]
    k = k_ref[...]
    v = v_ref[...].astype(F32)
    qd = (q * decay).astype(BF16)
    head = lambda a, h, n: a[:, h * n:(h + 1) * n]
    rows = jnp.concatenate([head(decay, h, dk) for h in range(heads)] + [head(k, h, dk) for h in range(heads)]
                           + [jnp.zeros((dk - 2 * heads * tb, dk), F32)], axis=0)
    cols = jnp.transpose(rows)
    for bi in range(tb):
        for h in range(heads):
            s_old = s_ref[bi, h]
            qh = head(q, h, dk)[bi:bi + 1]
            kh = head(k, h, dk)[bi:bi + 1]
            vh = head(v, h, dv)[bi:bi + 1]
            qk = jnp.sum(qh * kh, axis=-1, keepdims=True)
            qd8 = jnp.broadcast_to(head(qd, h, dk)[bi:bi + 1], (8, dk))
            o_h = jnp.dot(qd8, s_old.astype(BF16), preferred_element_type=F32)[:1, :] + qk * vh
            o_ref[bi:bi + 1, h * dv:(h + 1) * dv] = o_h
            c_decay = h * tb + bi
            c_key = (heads + h) * tb + bi
            snew_ref[bi, h] = s_old * cols[:, c_decay:c_decay + 1] + cols[:, c_key:c_key + 1] * vh


def _gla_decode(q, k, v, la, state, *, tb):
    nb, heads, dk, dv = state.shape
    row = lambda n: pl.BlockSpec((tb, n), lambda i: (i, 0))
    st = pl.BlockSpec((tb, heads, dk, dv), lambda i: (i, 0, 0, 0))
    return pl.pallas_call(
        functools.partial(_gla_decode_kernel, heads=heads, dk=dk, dv=dv),
        grid=(nb // tb,),
        in_specs=[row(heads * dk), row(heads * dk), row(heads * dv), row(heads * dk), st],
        out_specs=[row(heads * dv), st],
        out_shape=[jax.ShapeDtypeStruct((nb, heads * dv), F32), jax.ShapeDtypeStruct(state.shape, F32)],
        compiler_params=_params("parallel"),
        name="gla_decode",
    )(q, k, v, la, state)


def _mix_out_kernel(x_ref, lng_ref, lnb_ref, og_ref, g_ref, ol_ref, ng_ref, wv_ref, wo_ref, l1g_ref, l1b_ref,
                    h1_ref, mix_sc, *, alpha):
    gla_heads, _, dv = og_ref.shape
    mla_heads, _, vd = wv_ref.shape
    gw = gla_heads * dv
    gate = g_ref[...]
    for h in range(gla_heads):
        og = _rms_norm(og_ref[h], ng_ref[...])
        gh = gate[:, h * dv:(h + 1) * dv]
        mix_sc[:, h * dv:(h + 1) * dv] = (og * (gh * jax.nn.sigmoid(gh))).astype(BF16)
    for h in range(mla_heads):
        om = jnp.dot(ol_ref[h], wv_ref[h], preferred_element_type=F32)
        mix_sc[:, gw + h * vd:gw + (h + 1) * vd] = om.astype(BF16)
    mix = jnp.dot(mix_sc[...], wo_ref[...], preferred_element_type=F32)
    hres = _layer_norm(x_ref[...], lng_ref[...], lnb_ref[...])
    h1_ref[...] = _layer_norm(alpha * hres + mix, l1g_ref[...], l1b_ref[...])


def _mix_out(x, lng, lnb, og, gate, ol, ng, wv, wo, l1g, l1b, *, tm, alpha):
    m, d = x.shape
    gla_heads, _, dv = og.shape
    mla_heads, _, kvr = ol.shape
    row = lambda n: pl.BlockSpec((tm, n), lambda i: (i, 0))
    return pl.pallas_call(
        functools.partial(_mix_out_kernel, alpha=alpha),
        grid=(m // tm,),
        in_specs=[row(d), _full(lng.shape), _full(lnb.shape),
                  pl.BlockSpec((gla_heads, tm, dv), lambda i: (0, i, 0)), row(gate.shape[1]),
                  pl.BlockSpec((mla_heads, tm, kvr), lambda i: (0, i, 0)),
                  _full(ng.shape), _full(wv.shape), _full(wo.shape), _full(l1g.shape), _full(l1b.shape)],
        out_specs=row(d),
        out_shape=jax.ShapeDtypeStruct((m, d), F32),
        scratch_shapes=[pltpu.VMEM((tm, wo.shape[0]), BF16)],
        compiler_params=_params("parallel"),
        name="mix_out",
    )(x, lng, lnb, og, gate, ol, ng, wv, wo, l1g, l1b)


def _mlp_kernel(h_ref, wu_ref, wd_ref, l2g_ref, l2b_ref, o_ref, *rest, alpha):
    *w_bf16_refs, hb_sc = rest
    j = pl.program_id(1)

    @pl.when(j == 0)
    def _():
        hb_sc[...] = h_ref[...].astype(BF16)
        o_ref[...] = jnp.zeros_like(o_ref)

    wu, wd = wu_ref[...], wd_ref[...]
    if w_bf16_refs:
        wu, wd = wu.astype(BF16), wd.astype(BF16)
        w_bf16_refs[0][...] = wu
        w_bf16_refs[1][...] = wd
    u = jnp.maximum(jnp.dot(hb_sc[...], wu, preferred_element_type=F32), 0.0)
    o_ref[...] += jnp.dot((u * u).astype(BF16), wd, preferred_element_type=F32)

    @pl.when(j == pl.num_programs(1) - 1)
    def _():
        o_ref[...] = _layer_norm(alpha * h_ref[...] + o_ref[...], l2g_ref[...], l2b_ref[...])


def _mlp(h1, wu, wd, l2g, l2b, *, tm, tf, alpha):
    m, d = h1.shape
    ff = wu.shape[1]
    emit = wu.dtype != BF16
    assert not emit or m == tm
    wu_spec = pl.BlockSpec((d, tf), lambda i, j: (0, j))
    wd_spec = pl.BlockSpec((tf, d), lambda i, j: (j, 0))
    y_spec = pl.BlockSpec((tm, d), lambda i, j: (i, 0))
    y_shape = jax.ShapeDtypeStruct((m, d), F32)
    return pl.pallas_call(
        functools.partial(_mlp_kernel, alpha=alpha),
        grid=(m // tm, ff // tf),
        in_specs=[pl.BlockSpec((tm, d), lambda i, j: (i, 0)), wu_spec, wd_spec, _full(l2g.shape), _full(l2b.shape)],
        out_specs=[y_spec, wu_spec, wd_spec] if emit else y_spec,
        out_shape=([y_shape, jax.ShapeDtypeStruct(wu.shape, BF16), jax.ShapeDtypeStruct(wd.shape, BF16)]
                   if emit else y_shape),
        scratch_shapes=[pltpu.VMEM((tm, d), BF16)],
        compiler_params=_params("parallel", "arbitrary"),
        name="mlp",
    )(h1, wu, wd, l2g, l2b)


def _rope_tables(pos, rope):
    half = rope // 2
    inv_freq = ROPE_BASE ** (-jnp.arange(half, dtype=F32) / half)
    ang = pos.astype(F32)[:, None] * inv_freq[None, :]
    reps = LANES // half
    return jnp.tile(jnp.cos(ang), (1, reps)), jnp.tile(jnp.sin(ang), (1, reps))


def _rotate_half_cols(w):
    half = w.shape[-1] // 2
    return jnp.concatenate([-w[..., half:], w[..., :half]], axis=-1)


def _row_tile(m, cap):
    t = min(m, cap)
    while m % t:
        t //= 2
    return t


def kernel(x_prompt, x_sample, cache_ckv, cache_kpe, state_gla, page_table, meta_tokens, ln_emb_g, ln_emb_b, w_in, w_gla_a2, b_gla_a, gla_norm_g, mla_q_norm_g, w_mla_uq, mla_kv_norm_g, w_mla_ukv, w_out, ln1_g, ln1_b, w_up, w_down, ln2_g, ln2_b):
    bsz, seq, d = x_prompt.shape
    nb, dec_seq, _ = x_sample.shape
    depth = w_in.shape[0]
    assert depth == 1 and dec_seq == 1
    _, gla_heads, dk, dv = state_gla.shape[1:]
    page, kvr = cache_ckv.shape[2:]
    rope = cache_kpe.shape[3]
    n_pages = page_table.shape[1]
    n_meta = meta_tokens.shape[0]
    rank = w_gla_a2.shape[1]
    kw, gw = gla_heads * dk, gla_heads * dv
    qr = mla_q_norm_g.shape[-1]
    mla_heads = w_mla_ukv.shape[2]
    nope = w_mla_uq.shape[-1] // mla_heads - rope
    alpha = (2 * depth) ** 0.25
    mla_scale = (nope + rope) ** -0.5
    assert rank <= LANES and seq % GLA_CHUNK == 0 and n_pages % 2 == 0

    w0t = jnp.swapaxes(w_in[0], 0, 1).astype(BF16)
    o_ga = 2 * kw + 2 * gw
    o_cq = o_ga + rank
    o_kpe = o_cq + qr + kvr
    w_ga = jnp.pad(w0t[o_ga:o_cq], ((0, LANES - rank), (0, 0)))
    w_a2 = jnp.pad(w_gla_a2[0], ((0, LANES - rank), (0, 0))).astype(BF16)
    b_a = b_gla_a[0][None, :]
    w_mla = jnp.concatenate([w0t[o_cq:], _rotate_half_cols(w0t[o_kpe:].T).T], axis=0)
    uq = w_mla_uq[0].reshape(qr, mla_heads, nope + rope)
    uq_rope = uq[..., nope:]
    w_uq = jnp.concatenate([uq[..., :nope].reshape(qr, mla_heads * nope), uq_rope.reshape(qr, mla_heads * rope),
                            _rotate_half_cols(uq_rope).reshape(qr, mla_heads * rope)], axis=1).astype(BF16)
    ukv = w_mla_ukv[0]
    w_n = jnp.transpose(ukv[..., :nope], (1, 2, 0)).astype(BF16)
    w_v = jnp.transpose(ukv[..., nope:], (1, 0, 2)).astype(BF16)
    w_o = w_out[0].astype(BF16)
    lng, lnb = ln_emb_g[None, :], ln_emb_b[None, :]
    qg, kvg, ng = mla_q_norm_g, mla_kv_norm_g, gla_norm_g
    l1g, l1b, l2g, l2b = ln1_g, ln1_b, ln2_g, ln2_b

    assert LANES % rope == 0 and (mla_heads * rope) % LANES == 0
    cos_m, sin_m = _rope_tables(jnp.arange(n_meta, dtype=jnp.int32), rope)
    cos_p, sin_p = _rope_tables(n_meta + jnp.arange(seq, dtype=jnp.int32), rope)
    cos_s, sin_s = _rope_tables(jnp.full((nb,), n_pages * page, dtype=jnp.int32), rope)

    def project(x2d, cos8, sin8, tm, chunk):
        return _proj(x2d, lng, lnb, w0t, w_ga, w_a2, b_a, w_mla, qg, kvg, w_uq, w_n, cos8, sin8, tm=tm, kw=kw, gw=gw,
                     q_scale=dk ** -0.5, chunk=chunk, rope=rope, scale=mla_scale)

    def merge(x2d, o_gla, gate, o_lat, tm):
        return _mix_out(x2d, lng, lnb, o_gla, gate, o_lat, ng, w_v, w_o, l1g, l1b, tm=tm, alpha=alpha)

    xp = x_prompt.reshape(bsz * seq, d)
    (_, km, vm, _, bm), (_, ckv_m, kpe_m, kcat_m) = project(meta_tokens, cos_m, sin_m, n_meta, GLA_CHUNK)
    tm_p = _row_tile(seq, PROJ_ROW_TILE)
    (qg_p, kg_p, vg_p, gate_p, b_p), (q_p, ckv_p, kpe_p, kcat_p) = project(xp, cos_p, sin_p, tm_p, GLA_CHUNK)
    o_gla_p, s_p = _gla_prompt(qg_p, kg_p, vg_p, b_p, km, vm, bm,
                               bsz=bsz, seq=seq, heads=gla_heads, dk=dk, dv=dv)
    o_lat_p = _mla_prompt(q_p, kcat_p, kcat_m, bsz=bsz, seq=seq, tq=_row_tile(seq, MLA_PROMPT_BLOCK), kvr=kvr,
                          heads_per_block=min(mla_heads, MLA_PROMPT_HEADS_PER_BLOCK))
    h1_p = merge(xp, o_gla_p, gate_p, o_lat_p, tm_p)

    xs = x_sample.reshape(nb, d)
    (qg_s, kg_s, vg_s, gate_s, la_s), (q_s, ckv_s, kpe_s, _) = project(xs, cos_s, sin_s, nb, 1)
    o_gla_s, s_s = _gla_decode(qg_s, kg_s, vg_s, la_s, state_gla[0], tb=_row_tile(nb, 8))
    q_dec = jnp.transpose(q_s, (1, 0, 2)).astype(F32)
    cache_kpe_t = jnp.swapaxes(cache_kpe, 2, 3)
    o_lat_s = _mla_decode(page_table, q_dec, ckv_s[:, None, :], kpe_s[:, None, :], cache_ckv, cache_kpe_t)
    o_gla_s = jnp.transpose(o_gla_s.reshape(nb, gla_heads, dv), (1, 0, 2))
    o_lat_s = jnp.transpose(o_lat_s, (1, 0, 2)).astype(BF16)
    h1_s = merge(xs, o_gla_s, gate_s, o_lat_s, nb)

    y_s, w_u, w_d = _mlp(h1_s, w_up[0], w_down[0], l2g, l2b, tm=nb, tf=MLP_FF_TILE, alpha=alpha)
    y_p = _mlp(h1_p, w_u, w_d, l2g, l2b, tm=_row_tile(bsz * seq, MLP_ROW_TILE), tf=MLP_FF_TILE, alpha=alpha)

    def with_meta(meta_rows, rows):
        meta_b = jnp.broadcast_to(meta_rows[None], (bsz,) + meta_rows.shape)
        return jnp.concatenate([meta_b, rows.reshape(bsz, seq, rows.shape[-1])], axis=1)[None]

    return (y_p.reshape(bsz, seq, d), y_s.reshape(nb, 1, d),
            with_meta(ckv_m, ckv_p), with_meta(kpe_m, kpe_p), s_p[None],
            ckv_s.reshape(1, nb, 1, kvr), kpe_s.reshape(1, nb, 1, rope), s_s[None])
```
